```python
import math
import jax, jax.numpy as jnp
from jax import lax
import numpy as np

D_MODEL = 4096
BATCH = 2
SEQ = 8192
DEPTH = 1

HEAD_DIM = 128
N_HEADS_TOTAL = D_MODEL // HEAD_DIM
N_HEADS_SPARSE = N_HEADS_TOTAL // 2
N_KV_SPARSE = N_HEADS_SPARSE // 4
N_HEADS_SB = N_HEADS_TOTAL - N_HEADS_SPARSE
W_SPARSE = N_HEADS_SPARSE * HEAD_DIM
W_SPARSE_KV = N_KV_SPARSE * HEAD_DIM
W_SB = N_HEADS_SB * HEAD_DIM
MIX_WIDTH = W_SPARSE + W_SB
H_IDX = 32
D_IDX = 64
TOPK_MAX = 256
N_BUCKETS = 32
MAX_DISTANCE = 128
D_FF = 4 * D_MODEL
Q_BLOCK = 128
LN_EPS = 1e-5
COL_SIZES = (W_SPARSE, W_SPARSE_KV, W_SPARSE_KV, W_SB, W_SB, W_SB, H_IDX * D_IDX, D_IDX, H_IDX)
IN_COLS = W_SPARSE + 2 * W_SPARSE_KV + 3 * W_SB + H_IDX * D_IDX + D_IDX + H_IDX

kernel_name = "hybrid_dsa_stickbreak_deepnorm_adaln"


def layer_norm(x, g, b):
    xf = x.astype(jnp.float32)
    mu = jnp.mean(xf, axis=-1, keepdims=True)
    xc = xf - mu
    var = jnp.mean(xc * xc, axis=-1, keepdims=True)
    return (xc * lax.rsqrt(var + LN_EPS) * g + b).astype(x.dtype)


def rms_norm(x, g):
    xf = x.astype(jnp.float32)
    ms = jnp.mean(xf * xf, axis=-1, keepdims=True)
    return (xf * lax.rsqrt(ms + LN_EPS) * g).astype(x.dtype)


def t5_bucket(dist):
    n = jnp.maximum(dist, 0)
    max_exact = N_BUCKETS // 2
    nf = jnp.maximum(n, 1).astype(jnp.float32)
    large = max_exact + (jnp.log(nf / max_exact) / math.log(MAX_DISTANCE / max_exact)
                         * (N_BUCKETS - max_exact)).astype(jnp.int32)
    large = jnp.minimum(large, N_BUCKETS - 1)
    return jnp.where(n < max_exact, n, large)


def split_columns(proj):
    parts, start = [], 0
    for size in COL_SIZES:
        parts.append(proj[..., start:start + size])
        start += size
    return parts


def sparse_attention(q, k, v, iq, ik, iw, rel_bias, topk):
    B, S, H, Dh = q.shape
    G = k.shape[2]
    R = H // G
    pos = jnp.arange(S, dtype=jnp.int32)
    b_idx = jnp.arange(B)[:, None, None]

    def block(i):
        t0 = i * Q_BLOCK
        tq = t0 + jnp.arange(Q_BLOCK, dtype=jnp.int32)
        q_blk = lax.dynamic_slice_in_dim(q, t0, Q_BLOCK, axis=1)
        iq_blk = lax.dynamic_slice_in_dim(iq, t0, Q_BLOCK, axis=1)
        iw_blk = lax.dynamic_slice_in_dim(iw, t0, Q_BLOCK, axis=1)
        rel = jax.nn.relu(jnp.einsum('bqhd,bsd->bqhs', iq_blk, ik).astype(jnp.float32)
                          * (D_IDX ** -0.5))
        score = jnp.einsum('bqh,bqhs->bqs', iw_blk.astype(jnp.float32), rel)
        causal = pos[None, :] <= tq[:, None]
        score = jnp.where(causal[None], score, -jnp.inf)
        _, idx = lax.top_k(score, topk)
        valid = idx <= tq[None, :, None]
        k_sel = k[b_idx, idx]
        v_sel = v[b_idx, idx]
        qg = q_blk.reshape(B, Q_BLOCK, G, R, Dh)
        logits = jnp.einsum('bqgrd,bqkgd->bqgrk', qg, k_sel).astype(jnp.float32) * (Dh ** -0.5)
        bias = rel_bias[t5_bucket(tq[None, :, None] - idx)].astype(jnp.float32)
        bias = bias.reshape(B, Q_BLOCK, topk, G, R).transpose(0, 1, 3, 4, 2)
        logits = jnp.where(valid[:, :, None, None, :], logits + bias, -jnp.inf)
        p = jax.nn.softmax(logits, axis=-1).astype(v.dtype)
        o = jnp.einsum('bqgrk,bqkgd->bqgrd', p, v_sel)
        return o.reshape(B, Q_BLOCK, H * Dh)

    out = lax.map(block, jnp.arange(S // Q_BLOCK, dtype=jnp.int32))
    return out.transpose(1, 0, 2, 3).reshape(B, S, H * Dh)


def stick_breaking_attention(q, k, v):
    B, S, H, Dh = q.shape
    pos = jnp.arange(S, dtype=jnp.int32)

    def block(i):
        t0 = i * Q_BLOCK
        tq = t0 + jnp.arange(Q_BLOCK, dtype=jnp.int32)
        q_blk = lax.dynamic_slice_in_dim(q, t0, Q_BLOCK, axis=1)
        z = jnp.einsum('bqhd,bshd->bhqs', q_blk, k).astype(jnp.float32) * (Dh ** -0.5)
        strict = (pos[None, :] < tq[:, None])[None, None]
        log_beta = jax.nn.log_sigmoid(z)
        log_keep = jnp.where(strict, log_beta - z, 0.0)
        later = lax.cumsum(log_keep, axis=3, reverse=True) - log_keep
        a = jnp.where(strict, jnp.exp(log_beta + later), 0.0).astype(v.dtype)
        o = jnp.einsum('bhqs,bshd->bqhd', a, v)
        return o.reshape(B, Q_BLOCK, H * Dh)

    out = lax.map(block, jnp.arange(S // Q_BLOCK, dtype=jnp.int32))
    return out.transpose(1, 0, 2, 3).reshape(B, S, H * Dh)


def setup_inputs(seed: int = 0) -> dict:
    key = jax.random.key(seed)
    ks = jax.random.split(key, 20)
    f32 = jnp.float32
    beta_dn = (8.0 * DEPTH) ** -0.25
    nrm = lambda k, shape, s: jax.random.normal(k, shape, f32) * s
    gain = lambda k, shape: 1.0 + 0.02 * jax.random.normal(k, shape, f32)
    return {
        "x": nrm(ks[0], (BATCH, SEQ, D_MODEL), 1.0),
        "c": nrm(ks[1], (BATCH, D_MODEL), 1.0),
        "in_ln_g": gain(ks[2], (D_MODEL,)),
        "in_ln_b": nrm(ks[3], (D_MODEL,), 0.02),
        "rel_bias": nrm(ks[4], (N_BUCKETS, N_HEADS_SPARSE), 0.5),
        "w_ada": nrm(ks[5], (DEPTH, D_MODEL, 6 * D_MODEL), 0.5 * D_MODEL ** -0.5),
        "b_ada": nrm(ks[6], (DEPTH, 6 * D_MODEL), 0.02),
        "w_in": nrm(ks[7], (DEPTH, D_MODEL, IN_COLS), D_MODEL ** -0.5),
        "idx_kn_g": gain(ks[8], (DEPTH, D_IDX)),
        "idx_kn_b": nrm(ks[9], (DEPTH, D_IDX), 0.02),
        "gn_sparse_g": gain(ks[10], (DEPTH, W_SPARSE)),
        "gn_sb_g": gain(ks[11], (DEPTH, W_SB)),
        "w_out": nrm(ks[12], (DEPTH, MIX_WIDTH, D_MODEL), beta_dn * MIX_WIDTH ** -0.5),
        "ln1_g": gain(ks[13], (DEPTH, D_MODEL)),
        "ln1_b": nrm(ks[14], (DEPTH, D_MODEL), 0.02),
        "w_up": nrm(ks[15], (DEPTH, D_MODEL, D_FF), D_MODEL ** -0.5),
        "w_down": nrm(ks[16], (DEPTH, D_FF, D_MODEL), beta_dn * D_FF ** -0.5),
        "ln2_g": gain(ks[17], (DEPTH, D_MODEL)),
        "ln2_b": nrm(ks[18], (DEPTH, D_MODEL), 0.02),
    }


def reference(x, c, in_ln_g, in_ln_b, rel_bias, w_ada, b_ada, w_in, idx_kn_g, idx_kn_b,
              gn_sparse_g, gn_sb_g, w_out, ln1_g, ln1_b, w_up, w_down, ln2_g, ln2_b):
    B, S, _ = x.shape
    topk = min(TOPK_MAX, S // 4)
    alpha = (2.0 * DEPTH) ** 0.25
    h = layer_norm(x, in_ln_g, in_ln_b)
    cs = jax.nn.silu(c)
    for l in range(DEPTH):
        mod = cs @ w_ada[l] + b_ada[l]
        sh_m, sc_m, g_m, sh_f, sc_f, g_f = [m[:, None, :] for m in jnp.split(mod, 6, axis=-1)]
        u = h * (1.0 + sc_m) + sh_m
        proj = u @ w_in[l]
        aq, ak, av, bq, bk, bv, iq, ik, iw = split_columns(proj)
        aq = aq.reshape(B, S, N_HEADS_SPARSE, HEAD_DIM)
        ak = ak.reshape(B, S, N_KV_SPARSE, HEAD_DIM)
        av = av.reshape(B, S, N_KV_SPARSE, HEAD_DIM)
        bq = bq.reshape(B, S, N_HEADS_SB, HEAD_DIM)
        bk = bk.reshape(B, S, N_HEADS_SB, HEAD_DIM)
        bv = bv.reshape(B, S, N_HEADS_SB, HEAD_DIM)
        iq = iq.reshape(B, S, H_IDX, D_IDX)
        ik = layer_norm(ik, idx_kn_g[l], idx_kn_b[l])
        iw = iw * (H_IDX ** -0.5)
        o_a = sparse_attention(aq, ak, av, iq, ik, iw, rel_bias, topk)
        o_b = stick_breaking_attention(bq, bk, bv)
        mixed = jnp.concatenate([rms_norm(o_a, gn_sparse_g[l]), rms_norm(o_b, gn_sb_g[l])],
                                axis=-1) @ w_out[l]
        h = layer_norm(alpha * h + g_m * mixed, ln1_g[l], ln1_b[l])
        u = h * (1.0 + sc_f) + sh_f
        y = jnp.square(jax.nn.relu(u @ w_up[l])) @ w_down[l]
        h = layer_norm(alpha * h + g_f * y, ln2_g[l], ln2_b[l])
    return h
```

```python
import functools
import math

import jax
import jax.numpy as jnp
from jax import lax
from jax.experimental import pallas as pl
from jax.experimental.pallas import tpu as pltpu

HEAD_DIM = 128
H_IDX = 32
D_IDX = 64
TOPK_MAX = 256
N_BUCKETS = 32
MAX_DISTANCE = 128
LN_EPS = 1e-5

LANES = 128
SUBLANES = 8
VMEM_LIMIT = 56 * 1024 * 1024
NEG_BIG = -1e30
INT_MIN = -2147483648
SB_EXIT = -110.0

F32 = jnp.float32
BF16 = jnp.bfloat16


def _cparams(sem):
    return pltpu.CompilerParams(dimension_semantics=sem, vmem_limit_bytes=VMEM_LIMIT)


def _dot_nt(a, b):
    return lax.dot_general(a, b, (((1,), (1,)), ((), ())), preferred_element_type=F32)


def _ada_kernel(c_ref, w_ref, b_ref, o_ref, cs_ref, *, nb, d, tn):
    @pl.when(pl.program_id(0) == 0)
    def _():
        cv = c_ref[...]
        cs_ref[...] = cv * jax.nn.sigmoid(cv)

    nj = tn // LANES

    def body(k, accs):
        k8 = pl.multiple_of(k * SUBLANES, SUBLANES)
        out = list(accs)
        cs = [cs_ref[b, pl.ds(k8, SUBLANES), :] for b in range(nb)]
        for j in range(nj):
            wk = w_ref[pl.ds(k8, SUBLANES), j * LANES:(j + 1) * LANES]
            for b in range(nb):
                out[b * nj + j] = out[b * nj + j] + wk * cs[b]
        return tuple(out)

    init = tuple(jnp.zeros((SUBLANES, LANES), F32) for _ in range(nb * nj))
    accs = lax.fori_loop(0, d // SUBLANES, body, init, unroll=4)
    for b in range(nb):
        row = jnp.concatenate(
            [jnp.sum(accs[b * nj + j], axis=0, keepdims=True) for j in range(nj)], axis=1)
        o_ref[b:b + 1, :] = row + b_ref[...]


def _ada(c, w, bias):
    nb, d = c.shape
    n = w.shape[1]
    tn = 512 if n % 512 == 0 else LANES
    cb = jnp.broadcast_to(c[:, :, None], (nb, d, LANES))
    return pl.pallas_call(
        functools.partial(_ada_kernel, nb=nb, d=d, tn=tn),
        grid=(n // tn,),
        in_specs=[pl.BlockSpec((nb, d, LANES), lambda j: (0, 0, 0)),
                  pl.BlockSpec((d, tn), lambda j: (0, j)),
                  pl.BlockSpec((1, tn), lambda j: (0, j))],
        out_specs=pl.BlockSpec((nb, tn), lambda j: (0, j)),
        out_shape=jax.ShapeDtypeStruct((nb, n), F32),
        scratch_shapes=[pltpu.VMEM((nb, d, LANES), F32)],
        compiler_params=_cparams(("arbitrary",)),
        name="ada_mod",
    )(cb, w, bias.reshape(1, n))


def _layer_norm_rows(x, g, b):
    mu = jnp.mean(x, axis=-1, keepdims=True)
    xc = x - mu
    var = jnp.mean(xc * xc, axis=-1, keepdims=True)
    return xc * lax.rsqrt(var + LN_EPS) * g + b


def _ln_mod_kernel(x_ref, g_ref, b_ref, sc_ref, sh_ref, h_ref, u_ref):
    h = _layer_norm_rows(x_ref[...], g_ref[...], b_ref[...])
    h_ref[...] = h
    u_ref[...] = (h * (1.0 + sc_ref[0]) + sh_ref[0]).astype(BF16)


def _mod_spec(d, rows_per_batch_blocks, idx):
    return pl.BlockSpec((1, 1, d), lambda i: ((i // rows_per_batch_blocks) * 6 + idx, 0, 0))


def _ln_mod(x2, g, b, mod3, s, idx_sc, idx_sh):
    t, d = x2.shape
    tr = min(256, s)
    nbb = s // tr
    row = pl.BlockSpec((tr, d), lambda i: (i, 0))
    vec = pl.BlockSpec((1, d), lambda i: (0, 0))
    return pl.pallas_call(
        _ln_mod_kernel,
        grid=(t // tr,),
        in_specs=[row, vec, vec, _mod_spec(d, nbb, idx_sc), _mod_spec(d, nbb, idx_sh)],
        out_specs=[row, row],
        out_shape=[jax.ShapeDtypeStruct((t, d), F32), jax.ShapeDtypeStruct((t, d), BF16)],
        compiler_params=_cparams(("parallel",)),
        name="ln_mod",
    )(x2, g.reshape(1, d), b.reshape(1, d), mod3, mod3)


def _res_ln_kernel(h_ref, y_ref, gate_ref, g_ref, b_ref, sc_ref, sh_ref, *out_refs, alpha, with_u):
    h = _layer_norm_rows(alpha * h_ref[...] + gate_ref[0] * y_ref[...], g_ref[...], b_ref[...])
    out_refs[0][...] = h
    if with_u:
        out_refs[1][...] = (h * (1.0 + sc_ref[0]) + sh_ref[0]).astype(BF16)


def _res_ln(h, y, mod3, s, idx_gate, g, b, alpha, mod3_u=None, idx_sc=None, idx_sh=None):
    t, d = h.shape
    tr = min(256, s)
    nbb = s // tr
    with_u = mod3_u is not None
    if not with_u:
        mod3_u, idx_sc, idx_sh = mod3, idx_gate, idx_gate
    row = pl.BlockSpec((tr, d), lambda i: (i, 0))
    vec = pl.BlockSpec((1, d), lambda i: (0, 0))
    out_specs = [row, row] if with_u else [row]
    out_shape = [jax.ShapeDtypeStruct((t, d), F32)]
    if with_u:
        out_shape.append(jax.ShapeDtypeStruct((t, d), BF16))
    outs = pl.pallas_call(
        functools.partial(_res_ln_kernel, alpha=alpha, with_u=with_u),
        grid=(t // tr,),
        in_specs=[row, row, _mod_spec(d, nbb, idx_gate), vec, vec,
                  _mod_spec(d, nbb, idx_sc), _mod_spec(d, nbb, idx_sh)],
        out_specs=out_specs,
        out_shape=out_shape,
        compiler_params=_cparams(("parallel",)),
        name="res_ln",
    )(h, y, mod3, g.reshape(1, d), b.reshape(1, d), mod3_u, mod3_u)
    return outs if with_u else outs[0]


def _rms_cat_kernel(a_ref, b_ref, ga_ref, gb_ref, o_ref, *, wa):
    def rms(x, g):
        ms = jnp.mean(x * x, axis=-1, keepdims=True)
        return (x * lax.rsqrt(ms + LN_EPS) * g).astype(BF16)
    o_ref[:, :wa] = rms(a_ref[...], ga_ref[...])
    o_ref[:, wa:] = rms(b_ref[...], gb_ref[...])


def _rms_cat(oa, ob, ga, gb):
    t, wa = oa.shape
    wb = ob.shape[1]
    tr = min(512, t)
    return pl.pallas_call(
        functools.partial(_rms_cat_kernel, wa=wa),
        grid=(t // tr,),
        in_specs=[pl.BlockSpec((tr, wa), lambda i: (i, 0)), pl.BlockSpec((tr, wb), lambda i: (i, 0)),
                  pl.BlockSpec((1, wa), lambda i: (0, 0)), pl.BlockSpec((1, wb), lambda i: (0, 0))],
        out_specs=pl.BlockSpec((tr, wa + wb), lambda i: (i, 0)),
        out_shape=jax.ShapeDtypeStruct((t, wa + wb), BF16),
        compiler_params=_cparams(("parallel",)),
        name="rms_cat",
    )(oa, ob, ga.reshape(1, wa), gb.reshape(1, wb))


def _ikprep_kernel(t_ref, g_ref, b_ref, ke_ref, ko_ref):
    x = t_ref[...]
    lane = lax.broadcasted_iota(jnp.int32, x.shape, 1)
    is_k = lane < D_IDX
    mu = jnp.sum(jnp.where(is_k, x, 0.0), axis=-1, keepdims=True) * (1.0 / D_IDX)
    xc = jnp.where(is_k, x - mu, 0.0)
    var = jnp.sum(xc * xc, axis=-1, keepdims=True) * (1.0 / D_IDX)
    y = xc * lax.rsqrt(var + LN_EPS) * g_ref[...] + b_ref[...]
    ke_ref[...] = y.astype(BF16)
    ko_ref[...] = pltpu.roll(y, D_IDX, axis=1).astype(BF16)


def _ikprep(tail, g, b):
    t = tail.shape[0]
    tr = min(512, t)
    pad = lambda v: jnp.pad(v, (0, LANES - D_IDX)).reshape(1, LANES)
    row = pl.BlockSpec((tr, LANES), lambda i: (i, 0))
    vec = pl.BlockSpec((1, LANES), lambda i: (0, 0))
    return pl.pallas_call(
        _ikprep_kernel,
        grid=(t // tr,),
        in_specs=[row, vec, vec],
        out_specs=[row, row],
        out_shape=[jax.ShapeDtypeStruct((t, LANES), BF16)] * 2,
        compiler_params=_cparams(("parallel",)),
        name="ikprep",
    )(tail, pad(g), pad(b))


MM_TK_MAX = 4096


def _mm_kernel(a_ref, w_ref, o_ref, *, act, head_major, nk):
    r = jnp.dot(a_ref[...], w_ref[...], preferred_element_type=F32)
    if nk > 1:
        kk = pl.program_id(2)

        @pl.when(kk == 0)
        def _():
            o_ref[...] = r

        @pl.when(kk > 0)
        def _():
            o_ref[...] += r
        return
    if act == "relu2":
        r = jnp.square(jnp.maximum(r, 0.0))
    if head_major:
        for cblk in range(o_ref.shape[0]):
            o_ref[cblk] = r[:, cblk * LANES:(cblk + 1) * LANES].astype(o_ref.dtype)
    else:
        o_ref[...] = r.astype(o_ref.dtype)


def _mm_tiles(m, k, n):
    tm = min(1024, m)
    tn = 512 if n % 512 == 0 else LANES
    tk = k
    if k > MM_TK_MAX:
        tk = MM_TK_MAX // 2
        tn = 1024 if n % 1024 == 0 else tn
    return tm, min(tn, n), tk


def _matmul(a, w, out_dtype, act=None, head_major=False, name="matmul"):
    m, k = a.shape
    n = w.shape[1]
    tm, tn, tk = _mm_tiles(m, k, n)
    nk = k // tk
    assert m % tm == 0 and n % tn == 0 and k % tk == 0
    assert nk == 1 or (act is None and not head_major and out_dtype == F32)
    if head_major:
        out_spec = pl.BlockSpec((tn // LANES, tm, LANES), lambda i, j, kk: (j, i, 0))
        out_shape = jax.ShapeDtypeStruct((n // LANES, m, LANES), out_dtype)
    else:
        out_spec = pl.BlockSpec((tm, tn), lambda i, j, kk: (i, j))
        out_shape = jax.ShapeDtypeStruct((m, n), out_dtype)
    return pl.pallas_call(
        functools.partial(_mm_kernel, act=act, head_major=head_major, nk=nk),
        grid=(m // tm, n // tn, nk),
        in_specs=[pl.BlockSpec((tm, tk), lambda i, j, kk: (i, kk)),
                  pl.BlockSpec((tk, tn), lambda i, j, kk: (kk, j))],
        out_specs=out_spec,
        out_shape=out_shape,
        compiler_params=_cparams(("parallel", "arbitrary", "arbitrary")),
        name=name,
    )(a, w)


IDX_TQ = 128
IDX_LC = 512


def _indexer_kernel(iq_ref, ke_ref, ko_ref, tail_ref, o_ref, zs_ref, keys_ref, wb_ref,
                    *, topk, s, w_scale):
    tq, lc = IDX_TQ, IDX_LC
    npair = H_IDX // 2
    qi = pl.program_id(1)
    t0 = qi * tq

    tl = tail_ref[...]
    for h in range(H_IDX):
        col = tl[:, D_IDX + h:D_IDX + h + 1] * w_scale
        wb_ref[h] = jnp.broadcast_to(col, (tq, LANES))

    a = iq_ref[...].reshape(npair * tq, LANES)
    nchunks = (t0 + tq + lc - 1) // lc

    def chunk_body(c, carry):
        c0 = pl.multiple_of(c * lc, lc)
        zs_ref[0] = _dot_nt(a, ke_ref[pl.ds(c0, lc), :])
        zs_ref[1] = _dot_nt(a, ko_ref[pl.ds(c0, lc), :])
        for r in range(tq // SUBLANES):
            rows = slice(r * SUBLANES, (r + 1) * SUBLANES)
            acc = jnp.zeros((SUBLANES, lc), F32)
            for p in range(npair):
                for par in range(2):
                    z = zs_ref[par, p * tq + r * SUBLANES:p * tq + (r + 1) * SUBLANES, :]
                    w = jnp.tile(wb_ref[2 * p + par, rows, :], (1, lc // LANES))
                    acc = acc + jnp.maximum(z, 0.0) * w
            t_idx = t0 + r * SUBLANES + lax.broadcasted_iota(jnp.int32, (SUBLANES, lc), 0)
            s_idx = c0 + lax.broadcasted_iota(jnp.int32, (SUBLANES, lc), 1)
            bits = pltpu.bitcast(acc, jnp.int32)
            key = bits ^ ((bits >> 31) & jnp.int32(0x7FFFFFFF))
            keys_ref[rows, pl.ds(c0, lc)] = jnp.where(s_idx <= t_idx, key, jnp.int32(INT_MIN))
        return carry

    lax.fori_loop(0, nchunks, chunk_body, 0)

    nl = nchunks * (lc // LANES)

    def bit_body(i, ans):
        cand = ans + lax.shift_left(jnp.int32(1), jnp.int32(31) - i)

        def cnt_body(c, cnt):
            k = keys_ref[:, pl.ds(pl.multiple_of(c * LANES, LANES), LANES)]
            return cnt + jnp.where(k >= cand, 1.0, 0.0)

        cnt = lax.fori_loop(0, nl, cnt_body, jnp.zeros((tq, LANES), F32))
        tot = jnp.sum(cnt, axis=1, keepdims=True)
        return jnp.where(tot >= float(topk), cand, ans)

    ans = lax.fori_loop(0, 32, bit_body, jnp.full((tq, LANES), INT_MIN, jnp.int32))
    thr = jnp.maximum(ans, jnp.int32(INT_MIN + 1))

    def sel_body(c, carry):
        cs = pl.multiple_of(c * LANES, LANES)
        k = keys_ref[:, pl.ds(cs, LANES)]
        o_ref[:, pl.ds(cs, LANES)] = jnp.where(k >= thr, 0.0, NEG_BIG).astype(BF16)
        return carry

    def fill_body(c, carry):
        cs = pl.multiple_of(c * LANES, LANES)
        o_ref[:, pl.ds(cs, LANES)] = jnp.full((tq, LANES), NEG_BIG, BF16)
        return carry

    lax.fori_loop(0, nl, sel_body, 0)
    lax.fori_loop(nl, s // LANES, fill_body, 0)


def _indexer(p_main, ke, ko, tail, b, s, topk):
    tq = IDX_TQ
    nq = s // tq
    npair = H_IDX // 2
    w_scale = (H_IDX ** -0.5) * (D_IDX ** -0.5)
    return pl.pallas_call(
        functools.partial(_indexer_kernel, topk=topk, s=s, w_scale=w_scale),
        grid=(b, nq),
        in_specs=[pl.BlockSpec((npair, tq, LANES), lambda bi, qi: (0, bi * nq + qi, 0)),
                  pl.BlockSpec((s, LANES), lambda bi, qi: (bi, 0)),
                  pl.BlockSpec((s, LANES), lambda bi, qi: (bi, 0)),
                  pl.BlockSpec((tq, LANES), lambda bi, qi: (bi * nq + qi, 0))],
        out_specs=pl.BlockSpec((tq, s), lambda bi, qi: (bi * nq + qi, 0)),
        out_shape=jax.ShapeDtypeStruct((b * s, s), BF16),
        scratch_shapes=[pltpu.VMEM((2, npair * tq, IDX_LC), F32),
                        pltpu.VMEM((tq, s), jnp.int32),
                        pltpu.VMEM((H_IDX, tq, LANES), F32)],
        compiler_params=_cparams(("parallel", "arbitrary")),
        name="indexer_topk",
    )(p_main, ke, ko, tail)


def _sparse_kernel(q_ref, k_ref, v_ref, mb_ref, t0_ref, t1_ref, o_ref, m_ref, l_ref, acc_ref,
                   *, hs, g, ta):
    qi = pl.program_id(1)
    kj = pl.program_id(2)
    r = hs // g
    rows = r * ta

    @pl.when(kj == 0)
    def _():
        m_ref[...] = jnp.full(m_ref.shape, NEG_BIG, F32)
        l_ref[...] = jnp.zeros(l_ref.shape, F32)
        acc_ref[...] = jnp.zeros(acc_ref.shape, F32)

    def step(toep_ref):
        mb = mb_ref[...].astype(F32)
        for gi in range(g):
            sl = slice(gi * rows, (gi + 1) * rows)
            qg = q_ref[gi * r:(gi + 1) * r].reshape(rows, HEAD_DIM)
            sc = _dot_nt(qg, k_ref[gi]).reshape(r, ta, ta) + mb[None]
            if toep_ref is not None:
                sc = sc + toep_ref[gi * r:(gi + 1) * r]
            sc = sc.reshape(rows, ta)
            m_prev = m_ref[sl]
            m_next = jnp.maximum(m_prev, jnp.max(sc, axis=1, keepdims=True))
            p = jnp.exp(sc - jnp.tile(m_next, (1, ta // LANES)))
            alpha = jnp.exp(m_prev - m_next)
            l_ref[sl] = alpha * l_ref[sl] + jnp.sum(p, axis=1, keepdims=True)
            m_ref[sl] = m_next
            acc_ref[sl] = acc_ref[sl] * alpha + jnp.dot(p.astype(BF16), v_ref[gi],
                                                         preferred_element_type=F32)

    @pl.when(kj < qi - 1)
    def _():
        step(None)

    @pl.when(kj == qi - 1)
    def _():
        step(t1_ref)

    @pl.when(kj == qi)
    def _():
        step(t0_ref)
        for h in range(hs):
            hsl = slice(h * ta, (h + 1) * ta)
            o_ref[:, h * HEAD_DIM:(h + 1) * HEAD_DIM] = acc_ref[hsl] / l_ref[hsl]


def _sparse_attention(p_main, maskb, toep0, toep1, b, s, hs, g, off_q, off_k, off_v):
    ta = min(256, s)
    nq = s // ta
    assert off_q % hs == 0 and off_k % g == 0 and off_v % g == 0
    kidx = lambda qi, kj: jnp.minimum(kj, qi)
    return pl.pallas_call(
        functools.partial(_sparse_kernel, hs=hs, g=g, ta=ta),
        grid=(b, nq, nq),
        in_specs=[pl.BlockSpec((hs, ta, LANES), lambda bi, qi, kj: (off_q // hs, bi * nq + qi, 0)),
                  pl.BlockSpec((g, ta, LANES), lambda bi, qi, kj: (off_k // g, bi * nq + kidx(qi, kj), 0)),
                  pl.BlockSpec((g, ta, LANES), lambda bi, qi, kj: (off_v // g, bi * nq + kidx(qi, kj), 0)),
                  pl.BlockSpec((ta, ta), lambda bi, qi, kj: (bi * nq + qi, kidx(qi, kj))),
                  pl.BlockSpec((hs, ta, ta), lambda bi, qi, kj: (0, 0, 0)),
                  pl.BlockSpec((hs, ta, ta), lambda bi, qi, kj: (0, 0, 0))],
        out_specs=pl.BlockSpec((ta, hs * HEAD_DIM), lambda bi, qi, kj: (bi * nq + qi, 0)),
        out_shape=jax.ShapeDtypeStruct((b * s, hs * HEAD_DIM), F32),
        scratch_shapes=[pltpu.VMEM((hs * ta, LANES), F32)] * 3,
        compiler_params=_cparams(("parallel", "parallel", "arbitrary")),
        name="sparse_attn",
    )(p_main, p_main, p_main, maskb, toep0, toep1)


def _t5_bucket(dist):
    n = jnp.maximum(dist, 0)
    max_exact = N_BUCKETS // 2
    nf = jnp.maximum(n, 1).astype(F32)
    large = max_exact + (jnp.log(nf / max_exact) / math.log(MAX_DISTANCE / max_exact)
                         * (N_BUCKETS - max_exact)).astype(jnp.int32)
    large = jnp.minimum(large, N_BUCKETS - 1)
    return jnp.where(n < max_exact, n, large)


def _toeplitz_bias(rel_bias, ta):
    assert ta >= MAX_DISTANCE
    rr = jnp.arange(ta, dtype=jnp.int32)[:, None]
    cc = jnp.arange(ta, dtype=jnp.int32)[None, :]
    far = rel_bias[_t5_bucket(jnp.int32(2 * ta))]
    t0 = (rel_bias[_t5_bucket(rr - cc)] - far).transpose(2, 0, 1)
    t1 = (rel_bias[_t5_bucket(rr - cc + ta)] - far).transpose(2, 0, 1)
    return t0.astype(F32), t1.astype(F32)


SB_SUB = 128
SB_M = 4
SB_GS = 4


def _sb_kernel(q_ref, k_ref, v_ref, tri_ref, o_ref, acc_ref, run_ref, *, m, gs):
    n_sub = SB_SUB
    base = pl.program_id(2) * m
    row = lax.broadcasted_iota(jnp.int32, (n_sub, n_sub), 0)
    col = lax.broadcasted_iota(jnp.int32, (n_sub, n_sub), 1)
    strict = col < row

    def tile(hh, r, n, first):
        idx = hh * m + r
        kb_raw = base + r - n
        kb = jnp.maximum(kb_raw, 0)
        ks = pl.multiple_of(kb * n_sub, n_sub)
        q = q_ref[hh, r * n_sub:(r + 1) * n_sub, :]
        z = _dot_nt(q, k_ref[hh, pl.ds(ks, n_sub), :])
        sp = jnp.log1p(jnp.exp(-jnp.abs(z)))
        log_beta = jnp.minimum(z, 0.0) - sp
        log_keep = -jnp.maximum(z, 0.0) - sp
        if first:
            log_keep = jnp.where(strict, log_keep, 0.0)
        hi = log_keep.astype(BF16)
        lo = (log_keep - hi.astype(F32)).astype(BF16)
        tri = tri_ref[...]
        both = (jnp.dot(hi, tri, preferred_element_type=F32)
                + jnp.dot(lo, tri, preferred_element_type=F32))
        later = both[:, :n_sub]
        rowsum = both[:, n_sub:]
        if not first:
            later = later + run_ref[idx]
        a = jnp.exp(log_beta + later)
        if first:
            a = jnp.where(strict, a, 0.0)
        else:
            valid = kb_raw >= 0
            a = jnp.where(valid, a, 0.0)
            rowsum = jnp.where(valid, rowsum, 0.0)
        pv = jnp.dot(a.astype(BF16), v_ref[hh, pl.ds(ks, n_sub), :], preferred_element_type=F32)
        if first:
            acc_ref[idx] = pv
            run_ref[idx] = rowsum
        else:
            acc_ref[idx] = acc_ref[idx] + pv
            run_ref[idx] = run_ref[idx] + rowsum

    def sweep(n, first):
        for hh in range(gs):
            for r in range(m):
                tile(hh, r, n, first)
        mx = run_ref[0]
        for i in range(1, gs * m):
            mx = jnp.maximum(mx, run_ref[i])
        return jnp.max(mx)

    mx0 = sweep(0, True)

    def cond(carry):
        n, mx = carry
        return jnp.logical_and(n <= base + m - 1, mx > SB_EXIT)

    def body(carry):
        n, _ = carry
        return n + 1, sweep(n, False)

    lax.while_loop(cond, body, (jnp.int32(1), mx0))

    for hh in range(gs):
        for r in range(m):
            o_ref[r * n_sub:(r + 1) * n_sub, hh * HEAD_DIM:(hh + 1) * HEAD_DIM] = acc_ref[hh * m + r]


def _stick_breaking(p_main, b, s, hb, off_q, off_k, off_v):
    gs = SB_GS
    m = SB_M
    tq = SB_SUB * m
    assert s % tq == 0 and hb % gs == 0
    assert off_q % gs == 0 and off_k % gs == 0 and off_v % gs == 0
    nq = s // tq
    r = jnp.arange(SB_SUB)
    tri = jnp.concatenate([(r[:, None] > r[None, :]).astype(BF16),
                           jnp.ones((SB_SUB, SB_SUB), BF16)], axis=1)
    return pl.pallas_call(
        functools.partial(_sb_kernel, m=m, gs=gs),
        grid=(b, hb // gs, nq),
        in_specs=[pl.BlockSpec((gs, tq, LANES), lambda bi, gi, qi: (off_q // gs + gi, bi * nq + qi, 0)),
                  pl.BlockSpec((gs, s, LANES), lambda bi, gi, qi: (off_k // gs + gi, bi, 0)),
                  pl.BlockSpec((gs, s, LANES), lambda bi, gi, qi: (off_v // gs + gi, bi, 0)),
                  pl.BlockSpec((SB_SUB, 2 * SB_SUB), lambda bi, gi, qi: (0, 0))],
        out_specs=pl.BlockSpec((tq, gs * HEAD_DIM), lambda bi, gi, qi: (bi * nq + qi, gi)),
        out_shape=jax.ShapeDtypeStruct((b * s, hb * HEAD_DIM), F32),
        scratch_shapes=[pltpu.VMEM((gs * m, SB_SUB, LANES), F32)] * 2,
        compiler_params=_cparams(("parallel", "parallel", "arbitrary")),
        name="stick_breaking",
    )(p_main, p_main, p_main, tri)


def kernel(x, c, in_ln_g, in_ln_b, rel_bias, w_ada, b_ada, w_in, idx_kn_g, idx_kn_b,
           gn_sparse_g, gn_sb_g, w_out, ln1_g, ln1_b, w_up, w_down, ln2_g, ln2_b):
    b, s, d = x.shape
    depth = w_ada.shape[0]
    t = b * s
    n_heads = d // HEAD_DIM
    hs = n_heads // 2
    g = hs // 4
    hb = n_heads - hs
    ws, wkv, wsb = hs * HEAD_DIM, g * HEAD_DIM, hb * HEAD_DIM
    wiq = H_IDX * D_IDX
    topk = min(TOPK_MAX, s // 4)
    alpha = (2.0 * depth) ** 0.25
    qscale = HEAD_DIM ** -0.5

    off_iq = 0
    off_aq = wiq // LANES
    off_ak = off_aq + hs
    off_av = off_ak + g
    off_bq = off_av + g
    off_bk = off_bq + hb
    off_bv = off_bk + hb

    mods = [_ada(c, w_ada[l], b_ada[l]).reshape(b * 6, 1, d) for l in range(depth)]
    h, u = _ln_mod(x.reshape(t, d), in_ln_g, in_ln_b, mods[0], s, 1, 0)
    for l in range(depth):
        mod3 = mods[l]
        o = 0
        cols = {}
        for name, size in (("aq", ws), ("ak", wkv), ("av", wkv), ("bq", wsb), ("bk", wsb),
                           ("bv", wsb), ("iq", wiq), ("ik", D_IDX), ("iw", H_IDX)):
            cols[name] = w_in[l][:, o:o + size]
            o += size
        w_main = jnp.concatenate([cols["iq"], cols["aq"] * qscale, cols["ak"], cols["av"],
                                  cols["bq"] * qscale, cols["bk"], cols["bv"]], axis=1).astype(BF16)
        w_tail = jnp.pad(jnp.concatenate([cols["ik"], cols["iw"]], axis=1),
                         ((0, 0), (0, LANES - D_IDX - H_IDX))).astype(BF16)
        p_main = _matmul(u, w_main, BF16, head_major=True, name="in_proj")
        tail = _matmul(u, w_tail, F32, name="in_proj_tail")
        ke, ko = _ikprep(tail, idx_kn_g[l], idx_kn_b[l])
        maskb = _indexer(p_main, ke, ko, tail, b, s, topk)
        ta = min(256, s)
        toep0, toep1 = _toeplitz_bias(rel_bias, ta)
        o_a = _sparse_attention(p_main, maskb, toep0, toep1, b, s, hs, g, off_aq, off_ak, off_av)
        o_b = _stick_breaking(p_main, b, s, hb, off_bq, off_bk, off_bv)
        xn = _rms_cat(o_a, o_b, gn_sparse_g[l], gn_sb_g[l])
        mixed = _matmul(xn, w_out[l].astype(BF16), F32, name="out_proj")
        h, u = _res_ln(h, mixed, mod3, s, 2, ln1_g[l], ln1_b[l], alpha,
                       mod3_u=mod3, idx_sc=4, idx_sh=3)
        hmid = _matmul(u, w_up[l].astype(BF16), BF16, act="relu2", name="mlp_up")
        y = _matmul(hmid, w_down[l].astype(BF16), F32, name="mlp_down")
        if l + 1 < depth:
            h, u = _res_ln(h, y, mod3, s, 5, ln2_g[l], ln2_b[l], alpha,
                           mod3_u=mods[l + 1], idx_sc=1, idx_sh=0)
        else:
            h = _res_ln(h, y, mod3, s, 5, ln2_g[l], ln2_b[l], alpha)
    return h.reshape(b, s, d)
```

```python
import functools
import math

import jax
import jax.numpy as jnp
from jax import lax
from jax.experimental import pallas as pl
from jax.experimental.pallas import tpu as pltpu

HEAD_DIM = 128
H_IDX = 32
D_IDX = 64
TOPK_MAX = 256
N_BUCKETS = 32
MAX_DISTANCE = 128
LN_EPS = 1e-5

LANES = 128
SUBLANES = 8
VMEM_LIMIT = 56 * 1024 * 1024
NEG_BIG = -1e30
INT_MIN = -2147483648
SB_EXIT = -110.0

F32 = jnp.float32
BF16 = jnp.bfloat16


def _cparams(sem):
    return pltpu.CompilerParams(dimension_semantics=sem, vmem_limit_bytes=VMEM_LIMIT)


def _dot_nt(a, b):
    return lax.dot_general(a, b, (((1,), (1,)), ((), ())), preferred_element_type=F32)


def _ada_kernel(c_ref, w_ref, b_ref, o_ref, cs_ref, *, nb, d, tn):
    @pl.when(pl.program_id(0) == 0)
    def _():
        cv = c_ref[...]
        cs_ref[...] = cv * jax.nn.sigmoid(cv)

    nj = tn // LANES

    def body(k, accs):
        k8 = pl.multiple_of(k * SUBLANES, SUBLANES)
        out = list(accs)
        cs = [cs_ref[b, pl.ds(k8, SUBLANES), :] for b in range(nb)]
        for j in range(nj):
            wk = w_ref[pl.ds(k8, SUBLANES), j * LANES:(j + 1) * LANES]
            for b in range(nb):
                out[b * nj + j] = out[b * nj + j] + wk * cs[b]
        return tuple(out)

    init = tuple(jnp.zeros((SUBLANES, LANES), F32) for _ in range(nb * nj))
    accs = lax.fori_loop(0, d // SUBLANES, body, init, unroll=4)
    for b in range(nb):
        row = jnp.concatenate(
            [jnp.sum(accs[b * nj + j], axis=0, keepdims=True) for j in range(nj)], axis=1)
        o_ref[b:b + 1, :] = row + b_ref[...]


def _ada(c, w, bias):
    nb, d = c.shape
    n = w.shape[1]
    tn = 512 if n % 512 == 0 else LANES
    cb = jnp.broadcast_to(c[:, :, None], (nb, d, LANES))
    return pl.pallas_call(
        functools.partial(_ada_kernel, nb=nb, d=d, tn=tn),
        grid=(n // tn,),
        in_specs=[pl.BlockSpec((nb, d, LANES), lambda j: (0, 0, 0)),
                  pl.BlockSpec((d, tn), lambda j: (0, j)),
                  pl.BlockSpec((1, tn), lambda j: (0, j))],
        out_specs=pl.BlockSpec((nb, tn), lambda j: (0, j)),
        out_shape=jax.ShapeDtypeStruct((nb, n), F32),
        scratch_shapes=[pltpu.VMEM((nb, d, LANES), F32)],
        compiler_params=_cparams(("arbitrary",)),
        name="ada_mod",
    )(cb, w, bias.reshape(1, n))


def _layer_norm_rows(x, g, b):
    mu = jnp.mean(x, axis=-1, keepdims=True)
    xc = x - mu
    var = jnp.mean(xc * xc, axis=-1, keepdims=True)
    return xc * lax.rsqrt(var + LN_EPS) * g + b


def _ln_mod_kernel(x_ref, g_ref, b_ref, sc_ref, sh_ref, h_ref, u_ref):
    h = _layer_norm_rows(x_ref[...], g_ref[...], b_ref[...])
    h_ref[...] = h
    u_ref[...] = (h * (1.0 + sc_ref[0]) + sh_ref[0]).astype(BF16)


def _mod_spec(d, rows_per_batch_blocks, idx):
    return pl.BlockSpec((1, 1, d), lambda i: ((i // rows_per_batch_blocks) * 6 + idx, 0, 0))


def _ln_mod(x2, g, b, mod3, s, idx_sc, idx_sh):
    t, d = x2.shape
    tr = min(256, s)
    nbb = s // tr
    row = pl.BlockSpec((tr, d), lambda i: (i, 0))
    vec = pl.BlockSpec((1, d), lambda i: (0, 0))
    return pl.pallas_call(
        _ln_mod_kernel,
        grid=(t // tr,),
        in_specs=[row, vec, vec, _mod_spec(d, nbb, idx_sc), _mod_spec(d, nbb, idx_sh)],
        out_specs=[row, row],
        out_shape=[jax.ShapeDtypeStruct((t, d), F32), jax.ShapeDtypeStruct((t, d), BF16)],
        compiler_params=_cparams(("parallel",)),
        name="ln_mod",
    )(x2, g.reshape(1, d), b.reshape(1, d), mod3, mod3)


def _res_ln_kernel(h_ref, y_ref, gate_ref, g_ref, b_ref, sc_ref, sh_ref, *out_refs, alpha, with_u):
    h = _layer_norm_rows(alpha * h_ref[...] + gate_ref[0] * y_ref[...], g_ref[...], b_ref[...])
    out_refs[0][...] = h
    if with_u:
        out_refs[1][...] = (h * (1.0 + sc_ref[0]) + sh_ref[0]).astype(BF16)


def _res_ln(h, y, mod3, s, idx_gate, g, b, alpha, mod3_u=None, idx_sc=None, idx_sh=None):
    t, d = h.shape
    tr = min(256, s)
    nbb = s // tr
    with_u = mod3_u is not None
    if not with_u:
        mod3_u, idx_sc, idx_sh = mod3, idx_gate, idx_gate
    row = pl.BlockSpec((tr, d), lambda i: (i, 0))
    vec = pl.BlockSpec((1, d), lambda i: (0, 0))
    out_specs = [row, row] if with_u else [row]
    out_shape = [jax.ShapeDtypeStruct((t, d), F32)]
    if with_u:
        out_shape.append(jax.ShapeDtypeStruct((t, d), BF16))
    outs = pl.pallas_call(
        functools.partial(_res_ln_kernel, alpha=alpha, with_u=with_u),
        grid=(t // tr,),
        in_specs=[row, row, _mod_spec(d, nbb, idx_gate), vec, vec,
                  _mod_spec(d, nbb, idx_sc), _mod_spec(d, nbb, idx_sh)],
        out_specs=out_specs,
        out_shape=out_shape,
        compiler_params=_cparams(("parallel",)),
        name="res_ln",
    )(h, y, mod3, g.reshape(1, d), b.reshape(1, d), mod3_u, mod3_u)
    return outs if with_u else outs[0]


def _rms_cat_kernel(a_ref, b_ref, ga_ref, gb_ref, o_ref, *, wa):
    def rms(x, g):
        ms = jnp.mean(x * x, axis=-1, keepdims=True)
        return (x * lax.rsqrt(ms + LN_EPS) * g).astype(BF16)
    o_ref[:, :wa] = rms(a_ref[...], ga_ref[...])
    o_ref[:, wa:] = rms(b_ref[...], gb_ref[...])


def _rms_cat(oa, ob, ga, gb):
    t, wa = oa.shape
    wb = ob.shape[1]
    tr = min(512, t)
    return pl.pallas_call(
        functools.partial(_rms_cat_kernel, wa=wa),
        grid=(t // tr,),
        in_specs=[pl.BlockSpec((tr, wa), lambda i: (i, 0)), pl.BlockSpec((tr, wb), lambda i: (i, 0)),
                  pl.BlockSpec((1, wa), lambda i: (0, 0)), pl.BlockSpec((1, wb), lambda i: (0, 0))],
        out_specs=pl.BlockSpec((tr, wa + wb), lambda i: (i, 0)),
        out_shape=jax.ShapeDtypeStruct((t, wa + wb), BF16),
        compiler_params=_cparams(("parallel",)),
        name="rms_cat",
    )(oa, ob, ga.reshape(1, wa), gb.reshape(1, wb))


def _ikprep_kernel(t_ref, g_ref, b_ref, ke_ref, ko_ref):
    x = t_ref[...]
    lane = lax.broadcasted_iota(jnp.int32, x.shape, 1)
    is_k = lane < D_IDX
    mu = jnp.sum(jnp.where(is_k, x, 0.0), axis=-1, keepdims=True) * (1.0 / D_IDX)
    xc = jnp.where(is_k, x - mu, 0.0)
    var = jnp.sum(xc * xc, axis=-1, keepdims=True) * (1.0 / D_IDX)
    y = xc * lax.rsqrt(var + LN_EPS) * g_ref[...] + b_ref[...]
    ke_ref[...] = y.astype(BF16)
    ko_ref[...] = pltpu.roll(y, D_IDX, axis=1).astype(BF16)


def _ikprep(tail, g, b):
    t = tail.shape[0]
    tr = min(512, t)
    pad = lambda v: jnp.pad(v, (0, LANES - D_IDX)).reshape(1, LANES)
    row = pl.BlockSpec((tr, LANES), lambda i: (i, 0))
    vec = pl.BlockSpec((1, LANES), lambda i: (0, 0))
    return pl.pallas_call(
        _ikprep_kernel,
        grid=(t // tr,),
        in_specs=[row, vec, vec],
        out_specs=[row, row],
        out_shape=[jax.ShapeDtypeStruct((t, LANES), BF16)] * 2,
        compiler_params=_cparams(("parallel",)),
        name="ikprep",
    )(tail, pad(g), pad(b))


MM_TK_MAX = 4096


def _mm_kernel(a_ref, w_ref, o_ref, *, act, head_major, nk):
    r = jnp.dot(a_ref[...], w_ref[...], preferred_element_type=F32)
    if nk > 1:
        kk = pl.program_id(2)

        @pl.when(kk == 0)
        def _():
            o_ref[...] = r

        @pl.when(kk > 0)
        def _():
            o_ref[...] += r
        return
    if act == "relu2":
        r = jnp.square(jnp.maximum(r, 0.0))
    if head_major:
        for cblk in range(o_ref.shape[0]):
            o_ref[cblk] = r[:, cblk * LANES:(cblk + 1) * LANES].astype(o_ref.dtype)
    else:
        o_ref[...] = r.astype(o_ref.dtype)


def _mm_tiles(m, k, n):
    tm = min(1024, m)
    tn = 512 if n % 512 == 0 else LANES
    tk = k
    if k > MM_TK_MAX:
        tk = MM_TK_MAX // 2
        tn = 1024 if n % 1024 == 0 else tn
    return tm, min(tn, n), tk


def _matmul(a, w, out_dtype, act=None, head_major=False, name="matmul"):
    m, k = a.shape
    n = w.shape[1]
    tm, tn, tk = _mm_tiles(m, k, n)
    nk = k // tk
    assert m % tm == 0 and n % tn == 0 and k % tk == 0
    assert nk == 1 or (act is None and not head_major and out_dtype == F32)
    if head_major:
        out_spec = pl.BlockSpec((tn // LANES, tm, LANES), lambda i, j, kk: (j, i, 0))
        out_shape = jax.ShapeDtypeStruct((n // LANES, m, LANES), out_dtype)
    else:
        out_spec = pl.BlockSpec((tm, tn), lambda i, j, kk: (i, j))
        out_shape = jax.ShapeDtypeStruct((m, n), out_dtype)
    return pl.pallas_call(
        functools.partial(_mm_kernel, act=act, head_major=head_major, nk=nk),
        grid=(m // tm, n // tn, nk),
        in_specs=[pl.BlockSpec((tm, tk), lambda i, j, kk: (i, kk)),
                  pl.BlockSpec((tk, tn), lambda i, j, kk: (kk, j))],
        out_specs=out_spec,
        out_shape=out_shape,
        compiler_params=_cparams(("parallel", "arbitrary", "arbitrary")),
        name=name,
    )(a, w)


IDX_TQ = 128
IDX_LC = 512


def _indexer_kernel(iq_ref, ke_ref, ko_ref, tail_ref, o_ref, zs_ref, keys_ref, wb_ref,
                    *, topk, s, w_scale):
    tq, lc = IDX_TQ, IDX_LC
    npair = H_IDX // 2
    qi = pl.program_id(1)
    t0 = qi * tq

    tl = tail_ref[...]
    for h in range(H_IDX):
        col = tl[:, D_IDX + h:D_IDX + h + 1] * w_scale
        wb_ref[h] = jnp.broadcast_to(col, (tq, LANES))

    a = iq_ref[...].reshape(npair * tq, LANES)
    nchunks = (t0 + tq + lc - 1) // lc

    def chunk_body(c, carry):
        c0 = pl.multiple_of(c * lc, lc)
        zs_ref[0] = _dot_nt(a, ke_ref[pl.ds(c0, lc), :])
        zs_ref[1] = _dot_nt(a, ko_ref[pl.ds(c0, lc), :])
        for r in range(tq // SUBLANES):
            rows = slice(r * SUBLANES, (r + 1) * SUBLANES)
            acc = jnp.zeros((SUBLANES, lc), F32)
            for p in range(npair):
                for par in range(2):
                    z = zs_ref[par, p * tq + r * SUBLANES:p * tq + (r + 1) * SUBLANES, :]
                    w = jnp.tile(wb_ref[2 * p + par, rows, :], (1, lc // LANES))
                    acc = acc + jnp.maximum(z, 0.0) * w
            t_idx = t0 + r * SUBLANES + lax.broadcasted_iota(jnp.int32, (SUBLANES, lc), 0)
            s_idx = c0 + lax.broadcasted_iota(jnp.int32, (SUBLANES, lc), 1)
            bits = pltpu.bitcast(acc, jnp.int32)
            key = bits ^ ((bits >> 31) & jnp.int32(0x7FFFFFFF))
            keys_ref[rows, pl.ds(c0, lc)] = jnp.where(s_idx <= t_idx, key, jnp.int32(INT_MIN))
        return carry

    lax.fori_loop(0, nchunks, chunk_body, 0)

    nl = nchunks * (lc // LANES)

    def bit_body(i, ans):
        cand = ans + lax.shift_left(jnp.int32(1), jnp.int32(31) - i)

        def cnt_body(c, cnt):
            c0 = pl.multiple_of(c * lc, lc)
            for j in range(lc // LANES):
                k = keys_ref[:, pl.ds(c0 + j * LANES, LANES)]
                cnt = cnt + jnp.where(k >= cand, 1.0, 0.0)
            return cnt

        cnt = lax.fori_loop(0, nchunks, cnt_body, jnp.zeros((tq, LANES), F32))
        tot = jnp.sum(cnt, axis=1, keepdims=True)
        return jnp.where(tot >= float(topk), cand, ans)

    ans = lax.fori_loop(0, 32, bit_body, jnp.full((tq, LANES), INT_MIN, jnp.int32))
    thr = jnp.maximum(ans, jnp.int32(INT_MIN + 1))

    def sel_body(c, carry):
        cs = pl.multiple_of(c * LANES, LANES)
        k = keys_ref[:, pl.ds(cs, LANES)]
        o_ref[:, pl.ds(cs, LANES)] = jnp.where(k >= thr, 0.0, NEG_BIG).astype(BF16)
        return carry

    def fill_body(c, carry):
        cs = pl.multiple_of(c * LANES, LANES)
        o_ref[:, pl.ds(cs, LANES)] = jnp.full((tq, LANES), NEG_BIG, BF16)
        return carry

    lax.fori_loop(0, nl, sel_body, 0)
    lax.fori_loop(nl, s // LANES, fill_body, 0)


def _indexer(p_main, ke, ko, tail, b, s, topk):
    tq = IDX_TQ
    nq = s // tq
    npair = H_IDX // 2
    w_scale = (H_IDX ** -0.5) * (D_IDX ** -0.5)
    return pl.pallas_call(
        functools.partial(_indexer_kernel, topk=topk, s=s, w_scale=w_scale),
        grid=(b, nq),
        in_specs=[pl.BlockSpec((npair, tq, LANES), lambda bi, qi: (0, bi * nq + qi, 0)),
                  pl.BlockSpec((s, LANES), lambda bi, qi: (bi, 0)),
                  pl.BlockSpec((s, LANES), lambda bi, qi: (bi, 0)),
                  pl.BlockSpec((tq, LANES), lambda bi, qi: (bi * nq + qi, 0))],
        out_specs=pl.BlockSpec((tq, s), lambda bi, qi: (bi * nq + qi, 0)),
        out_shape=jax.ShapeDtypeStruct((b * s, s), BF16),
        scratch_shapes=[pltpu.VMEM((2, npair * tq, IDX_LC), F32),
                        pltpu.VMEM((tq, s), jnp.int32),
                        pltpu.VMEM((H_IDX, tq, LANES), F32)],
        compiler_params=_cparams(("parallel", "arbitrary")),
        name="indexer_topk",
    )(p_main, ke, ko, tail)


SP_TA = 256


def _sparse_kernel(qt_ref, kt_ref, q_ref, k_ref, v_ref, mb_ref, t0_ref, t1_ref, o_ref,
                   m_ref, l_ref, acc_ref, *, hs, g, ta):
    pair = pl.program_id(1)
    qi = qt_ref[pair]
    kj = kt_ref[pair]
    r = hs // g
    rows = r * ta

    @pl.when(kj == 0)
    def _():
        m_ref[...] = jnp.full(m_ref.shape, NEG_BIG, F32)
        l_ref[...] = jnp.zeros(l_ref.shape, F32)
        acc_ref[...] = jnp.zeros(acc_ref.shape, F32)

    def step(toep_ref):
        mb = mb_ref[...].astype(F32)
        for gi in range(g):
            sl = slice(gi * rows, (gi + 1) * rows)
            qg = q_ref[gi * r:(gi + 1) * r].reshape(rows, HEAD_DIM)
            sc = _dot_nt(qg, k_ref[gi]).reshape(r, ta, ta) + mb[None]
            if toep_ref is not None:
                sc = sc + toep_ref[gi * r:(gi + 1) * r]
            sc = sc.reshape(rows, ta)
            m_prev = m_ref[sl]
            m_next = jnp.maximum(m_prev, jnp.max(sc, axis=1, keepdims=True))
            p = jnp.exp(sc - jnp.tile(m_next, (1, ta // LANES)))
            alpha = jnp.exp(m_prev - m_next)
            l_ref[sl] = alpha * l_ref[sl] + jnp.sum(p, axis=1, keepdims=True)
            m_ref[sl] = m_next
            acc_ref[sl] = acc_ref[sl] * alpha + jnp.dot(p.astype(BF16), v_ref[gi],
                                                         preferred_element_type=F32)

    @pl.when(kj < qi - 1)
    def _():
        step(None)

    @pl.when(kj == qi - 1)
    def _():
        step(t1_ref)

    @pl.when(kj == qi)
    def _():
        step(t0_ref)
        for h in range(hs):
            hsl = slice(h * ta, (h + 1) * ta)
            o_ref[:, h * HEAD_DIM:(h + 1) * HEAD_DIM] = acc_ref[hsl] / l_ref[hsl]


def _sparse_attention(p_main, maskb, toep0, toep1, b, s, hs, g, off_q, off_k, off_v):
    ta = min(SP_TA, s)
    nq = s // ta
    assert off_q % hs == 0 and off_k % g == 0 and off_v % g == 0
    pairs = [(qi, kj) for qi in range(nq) for kj in range(qi + 1)]
    qt = jnp.asarray([p[0] for p in pairs], jnp.int32)
    kt = jnp.asarray([p[1] for p in pairs], jnp.int32)
    grid_spec = pltpu.PrefetchScalarGridSpec(
        num_scalar_prefetch=2,
        grid=(b, len(pairs)),
        in_specs=[pl.BlockSpec((hs, ta, LANES), lambda bi, p, qt, kt: (off_q // hs, bi * nq + qt[p], 0)),
                  pl.BlockSpec((g, ta, LANES), lambda bi, p, qt, kt: (off_k // g, bi * nq + kt[p], 0)),
                  pl.BlockSpec((g, ta, LANES), lambda bi, p, qt, kt: (off_v // g, bi * nq + kt[p], 0)),
                  pl.BlockSpec((ta, ta), lambda bi, p, qt, kt: (bi * nq + qt[p], kt[p])),
                  pl.BlockSpec((hs, ta, ta), lambda bi, p, qt, kt: (0, 0, 0)),
                  pl.BlockSpec((hs, ta, ta), lambda bi, p, qt, kt: (0, 0, 0))],
        out_specs=pl.BlockSpec((ta, hs * HEAD_DIM), lambda bi, p, qt, kt: (bi * nq + qt[p], 0)),
        scratch_shapes=[pltpu.VMEM((hs * ta, LANES), F32)] * 3,
    )
    return pl.pallas_call(
        functools.partial(_sparse_kernel, hs=hs, g=g, ta=ta),
        grid_spec=grid_spec,
        out_shape=jax.ShapeDtypeStruct((b * s, hs * HEAD_DIM), F32),
        compiler_params=_cparams(("parallel", "arbitrary")),
        name="sparse_attn",
    )(qt, kt, p_main, p_main, p_main, maskb, toep0, toep1)


def _t5_bucket(dist):
    n = jnp.maximum(dist, 0)
    max_exact = N_BUCKETS // 2
    nf = jnp.maximum(n, 1).astype(F32)
    large = max_exact + (jnp.log(nf / max_exact) / math.log(MAX_DISTANCE / max_exact)
                         * (N_BUCKETS - max_exact)).astype(jnp.int32)
    large = jnp.minimum(large, N_BUCKETS - 1)
    return jnp.where(n < max_exact, n, large)


def _toeplitz_bias(rel_bias, ta):
    assert ta >= MAX_DISTANCE
    nh = rel_bias.shape[1]
    rr = jnp.arange(ta, dtype=jnp.int32)[:, None]
    cc = jnp.arange(ta, dtype=jnp.int32)[None, :]
    far = rel_bias[_t5_bucket(jnp.int32(2 * ta))]
    shifted = (rel_bias - far[None, :]).T

    def table(dist):
        onehot = (_t5_bucket(dist).reshape(1, ta * ta)
                  == jnp.arange(N_BUCKETS, dtype=jnp.int32)[:, None]).astype(F32)
        return jnp.dot(shifted, onehot, precision=lax.Precision.HIGHEST).reshape(nh, ta, ta)

    return table(rr - cc), table(rr - cc + ta)


SB_SUB = 128
SB_M = 4
SB_GS = 4
SB_CH = 64


def _sb_kernel(q_ref, k_ref, v_ref, tri_ref, o_ref, z_ref, lb_ref, hl_ref, both_ref, a_ref,
               acc_ref, run_ref, *, m, gs):
    n_sub = SB_SUB
    ch = SB_CH
    rows = gs * m * n_sub
    nch = rows // ch
    base = pl.program_id(2) * m

    def key_start(r, n):
        return pl.multiple_of(jnp.maximum(base + r - n, 0) * n_sub, n_sub)

    def strict_mask(r0):
        row = (r0 % n_sub) + lax.broadcasted_iota(jnp.int32, (ch, n_sub), 0)
        col = lax.broadcasted_iota(jnp.int32, (ch, n_sub), 1)
        return col < row

    def sweep(n, first):
        for hh in range(gs):
            for r in range(m):
                i = hh * m + r
                q = q_ref[hh, r * n_sub:(r + 1) * n_sub, :]
                z_ref[i * n_sub:(i + 1) * n_sub, :] = _dot_nt(q, k_ref[hh, pl.ds(key_start(r, n), n_sub), :])

        def split_body(c, carry):
            r0 = pl.multiple_of(c * ch, ch)
            z = z_ref[pl.ds(r0, ch), :]
            sp = jnp.log(1.0 + jnp.exp(-jnp.abs(z)))
            lb_ref[pl.ds(r0, ch), :] = jnp.minimum(z, 0.0) - sp
            log_keep = -jnp.maximum(z, 0.0) - sp
            if first:
                log_keep = jnp.where(strict_mask(r0), log_keep, 0.0)
            hi = log_keep.astype(BF16)
            hl_ref[pl.ds(r0, ch), :] = hi
            hl_ref[pl.ds(rows + r0, ch), :] = (log_keep - hi.astype(F32)).astype(BF16)
            return carry

        lax.fori_loop(0, nch, split_body, 0, unroll=2)

        both_ref[...] = jnp.dot(hl_ref[...], tri_ref[...], preferred_element_type=F32)

        def weight_body(c, mx):
            r0 = pl.multiple_of(c * ch, ch)
            both = both_ref[pl.ds(r0, ch), :] + both_ref[pl.ds(rows + r0, ch), :]
            later = both[:, :n_sub]
            rowsum = both[:, n_sub:]
            if first:
                run = rowsum
            else:
                valid = base + (r0 // n_sub) % m - n >= 0
                prev = run_ref[pl.ds(r0, ch), :]
                later = later + prev
                run = prev + jnp.where(valid, rowsum, 0.0)
            a = jnp.exp(lb_ref[pl.ds(r0, ch), :] + later)
            if first:
                a = jnp.where(strict_mask(r0), a, 0.0)
            else:
                a = jnp.where(valid, a, 0.0)
            a_ref[pl.ds(r0, ch), :] = a.astype(BF16)
            run_ref[pl.ds(r0, ch), :] = run
            return jnp.maximum(mx, run)

        mx = lax.fori_loop(0, nch, weight_body, jnp.full((ch, n_sub), -jnp.inf, F32), unroll=2)

        for hh in range(gs):
            for r in range(m):
                i = hh * m + r
                sl = slice(i * n_sub, (i + 1) * n_sub)
                pv = jnp.dot(a_ref[sl, :], v_ref[hh, pl.ds(key_start(r, n), n_sub), :],
                             preferred_element_type=F32)
                if first:
                    acc_ref[sl, :] = pv
                else:
                    acc_ref[sl, :] = acc_ref[sl, :] + pv
        return jnp.max(mx)

    mx0 = sweep(0, True)

    def cond(carry):
        n, mx = carry
        return jnp.logical_and(n <= base + m - 1, mx > SB_EXIT)

    def body(carry):
        n, _ = carry
        return n + 1, sweep(n, False)

    lax.while_loop(cond, body, (jnp.int32(1), mx0))

    for hh in range(gs):
        for r in range(m):
            i = hh * m + r
            o_ref[r * n_sub:(r + 1) * n_sub, hh * HEAD_DIM:(hh + 1) * HEAD_DIM] = (
                acc_ref[i * n_sub:(i + 1) * n_sub, :])


def _stick_breaking(p_main, b, s, hb, off_q, off_k, off_v):
    gs = SB_GS
    m = SB_M
    tq = SB_SUB * m
    assert s % tq == 0 and hb % gs == 0
    assert off_q % gs == 0 and off_k % gs == 0 and off_v % gs == 0
    nq = s // tq
    rows = gs * m * SB_SUB
    idx = jnp.arange(SB_SUB)
    tri = jnp.concatenate([(idx[:, None] > idx[None, :]).astype(BF16),
                           jnp.ones((SB_SUB, SB_SUB), BF16)], axis=1)
    return pl.pallas_call(
        functools.partial(_sb_kernel, m=m, gs=gs),
        grid=(b, hb // gs, nq),
        in_specs=[pl.BlockSpec((gs, tq, LANES), lambda bi, gi, qi: (off_q // gs + gi, bi * nq + qi, 0)),
                  pl.BlockSpec((gs, s, LANES), lambda bi, gi, qi: (off_k // gs + gi, bi, 0)),
                  pl.BlockSpec((gs, s, LANES), lambda bi, gi, qi: (off_v // gs + gi, bi, 0)),
                  pl.BlockSpec((SB_SUB, 2 * SB_SUB), lambda bi, gi, qi: (0, 0))],
        out_specs=pl.BlockSpec((tq, gs * HEAD_DIM), lambda bi, gi, qi: (bi * nq + qi, gi)),
        out_shape=jax.ShapeDtypeStruct((b * s, hb * HEAD_DIM), F32),
        scratch_shapes=[pltpu.VMEM((rows, LANES), F32),
                        pltpu.VMEM((rows, LANES), F32),
                        pltpu.VMEM((2 * rows, LANES), BF16),
                        pltpu.VMEM((2 * rows, 2 * LANES), F32),
                        pltpu.VMEM((rows, LANES), BF16),
                        pltpu.VMEM((rows, LANES), F32),
                        pltpu.VMEM((rows, LANES), F32)],
        compiler_params=_cparams(("parallel", "parallel", "arbitrary")),
        name="stick_breaking",
    )(p_main, p_main, p_main, tri)


def kernel(x, c, in_ln_g, in_ln_b, rel_bias, w_ada, b_ada, w_in, idx_kn_g, idx_kn_b,
           gn_sparse_g, gn_sb_g, w_out, ln1_g, ln1_b, w_up, w_down, ln2_g, ln2_b):
    b, s, d = x.shape
    depth = w_ada.shape[0]
    t = b * s
    n_heads = d // HEAD_DIM
    hs = n_heads // 2
    g = hs // 4
    hb = n_heads - hs
    ws, wkv, wsb = hs * HEAD_DIM, g * HEAD_DIM, hb * HEAD_DIM
    wiq = H_IDX * D_IDX
    topk = min(TOPK_MAX, s // 4)
    alpha = (2.0 * depth) ** 0.25
    qscale = HEAD_DIM ** -0.5

    off_aq = wiq // LANES
    off_ak = off_aq + hs
    off_av = off_ak + g
    off_bq = off_av + g
    off_bk = off_bq + hb
    off_bv = off_bk + hb

    mods = [_ada(c, w_ada[l], b_ada[l]).reshape(b * 6, 1, d) for l in range(depth)]
    h, u = _ln_mod(x.reshape(t, d), in_ln_g, in_ln_b, mods[0], s, 1, 0)
    toep0, toep1 = _toeplitz_bias(rel_bias, min(SP_TA, s))
    for l in range(depth):
        mod3 = mods[l]
        o = 0
        cols = {}
        for name, size in (("aq", ws), ("ak", wkv), ("av", wkv), ("bq", wsb), ("bk", wsb),
                           ("bv", wsb), ("iq", wiq), ("ik", D_IDX), ("iw", H_IDX)):
            cols[name] = w_in[l][:, o:o + size]
            o += size
        w_main = jnp.concatenate([cols["iq"], cols["aq"] * qscale, cols["ak"], cols["av"],
                                  cols["bq"] * qscale, cols["bk"], cols["bv"]], axis=1).astype(BF16)
        w_tail = jnp.pad(jnp.concatenate([cols["ik"], cols["iw"]], axis=1),
                         ((0, 0), (0, LANES - D_IDX - H_IDX))).astype(BF16)
        p_main = _matmul(u, w_main, BF16, head_major=True, name="in_proj")
        tail = _matmul(u, w_tail, F32, name="in_proj_tail")
        ke, ko = _ikprep(tail, idx_kn_g[l], idx_kn_b[l])
        maskb = _indexer(p_main, ke, ko, tail, b, s, topk)
        o_a = _sparse_attention(p_main, maskb, toep0, toep1, b, s, hs, g, off_aq, off_ak, off_av)
        o_b = _stick_breaking(p_main, b, s, hb, off_bq, off_bk, off_bv)
        xn = _rms_cat(o_a, o_b, gn_sparse_g[l], gn_sb_g[l])
        mixed = _matmul(xn, w_out[l].astype(BF16), F32, name="out_proj")
        h, u = _res_ln(h, mixed, mod3, s, 2, ln1_g[l], ln1_b[l], alpha,
                       mod3_u=mod3, idx_sc=4, idx_sh=3)
        hmid = _matmul(u, w_up[l].astype(BF16), BF16, act="relu2", name="mlp_up")
        y = _matmul(hmid, w_down[l].astype(BF16), F32, name="mlp_down")
        if l + 1 < depth:
            h, u = _res_ln(h, y, mod3, s, 5, ln2_g[l], ln2_b[l], alpha,
                           mod3_u=mods[l + 1], idx_sc=1, idx_sh=0)
        else:
            h = _res_ln(h, y, mod3, s, 5, ln2_g[l], ln2_b[l], alpha)
    return h.reshape(b, s, d)
```

```python
import functools
import math

import jax
import jax.numpy as jnp
from jax import lax
from jax.experimental import pallas as pl
from jax.experimental.pallas import tpu as pltpu

HEAD_DIM = 128
H_IDX = 32
D_IDX = 64
TOPK_MAX = 256
N_BUCKETS = 32
MAX_DISTANCE = 128
LN_EPS = 1e-5

LANES = 128
SUBLANES = 8
VMEM_LIMIT = 56 * 1024 * 1024
NEG_BIG = -1e30
INT_MIN = -2147483648
SB_EXIT = -110.0

F32 = jnp.float32
BF16 = jnp.bfloat16


def _cparams(sem):
    return pltpu.CompilerParams(dimension_semantics=sem, vmem_limit_bytes=VMEM_LIMIT)


def _dot_nt(a, b):
    return lax.dot_general(a, b, (((1,), (1,)), ((), ())), preferred_element_type=F32)


def _ada_kernel(c_ref, w_ref, b_ref, o_ref, cs_ref, *, nb, d, tn):
    @pl.when(pl.program_id(0) == 0)
    def _():
        cv = c_ref[...]
        cs_ref[...] = cv * jax.nn.sigmoid(cv)

    nj = tn // LANES

    def body(k, accs):
        k8 = pl.multiple_of(k * SUBLANES, SUBLANES)
        out = list(accs)
        cs = [cs_ref[b, pl.ds(k8, SUBLANES), :] for b in range(nb)]
        for j in range(nj):
            wk = w_ref[pl.ds(k8, SUBLANES), j * LANES:(j + 1) * LANES]
            for b in range(nb):
                out[b * nj + j] = out[b * nj + j] + wk * cs[b]
        return tuple(out)

    init = tuple(jnp.zeros((SUBLANES, LANES), F32) for _ in range(nb * nj))
    accs = lax.fori_loop(0, d // SUBLANES, body, init, unroll=4)
    for b in range(nb):
        row = jnp.concatenate(
            [jnp.sum(accs[b * nj + j], axis=0, keepdims=True) for j in range(nj)], axis=1)
        o_ref[b:b + 1, :] = row + b_ref[...]


def _ada(c, w, bias):
    nb, d = c.shape
    n = w.shape[1]
    tn = 512 if n % 512 == 0 else LANES
    cb = jnp.broadcast_to(c[:, :, None], (nb, d, LANES))
    return pl.pallas_call(
        functools.partial(_ada_kernel, nb=nb, d=d, tn=tn),
        grid=(n // tn,),
        in_specs=[pl.BlockSpec((nb, d, LANES), lambda j: (0, 0, 0)),
                  pl.BlockSpec((d, tn), lambda j: (0, j)),
                  pl.BlockSpec((1, tn), lambda j: (0, j))],
        out_specs=pl.BlockSpec((nb, tn), lambda j: (0, j)),
        out_shape=jax.ShapeDtypeStruct((nb, n), F32),
        scratch_shapes=[pltpu.VMEM((nb, d, LANES), F32)],
        compiler_params=_cparams(("arbitrary",)),
        name="ada_mod",
    )(cb, w, bias.reshape(1, n))


def _layer_norm_rows(x, g, b):
    mu = jnp.mean(x, axis=-1, keepdims=True)
    xc = x - mu
    var = jnp.mean(xc * xc, axis=-1, keepdims=True)
    return xc * lax.rsqrt(var + LN_EPS) * g + b


def _ln_mod_kernel(x_ref, g_ref, b_ref, sc_ref, sh_ref, h_ref, u_ref):
    h = _layer_norm_rows(x_ref[...], g_ref[...], b_ref[...])
    h_ref[...] = h
    u_ref[...] = (h * (1.0 + sc_ref[0]) + sh_ref[0]).astype(BF16)


def _mod_spec(d, rows_per_batch_blocks, idx):
    return pl.BlockSpec((1, 1, d), lambda i: ((i // rows_per_batch_blocks) * 6 + idx, 0, 0))


def _ln_mod(x2, g, b, mod3, s, idx_sc, idx_sh):
    t, d = x2.shape
    tr = min(256, s)
    nbb = s // tr
    row = pl.BlockSpec((tr, d), lambda i: (i, 0))
    vec = pl.BlockSpec((1, d), lambda i: (0, 0))
    return pl.pallas_call(
        _ln_mod_kernel,
        grid=(t // tr,),
        in_specs=[row, vec, vec, _mod_spec(d, nbb, idx_sc), _mod_spec(d, nbb, idx_sh)],
        out_specs=[row, row],
        out_shape=[jax.ShapeDtypeStruct((t, d), F32), jax.ShapeDtypeStruct((t, d), BF16)],
        compiler_params=_cparams(("parallel",)),
        name="ln_mod",
    )(x2, g.reshape(1, d), b.reshape(1, d), mod3, mod3)


def _res_ln_kernel(h_ref, y_ref, gate_ref, g_ref, b_ref, sc_ref, sh_ref, *out_refs, alpha, with_u):
    h = _layer_norm_rows(alpha * h_ref[...] + gate_ref[0] * y_ref[...], g_ref[...], b_ref[...])
    out_refs[0][...] = h
    if with_u:
        out_refs[1][...] = (h * (1.0 + sc_ref[0]) + sh_ref[0]).astype(BF16)


def _res_ln(h, y, mod3, s, idx_gate, g, b, alpha, mod3_u=None, idx_sc=None, idx_sh=None):
    t, d = h.shape
    tr = min(256, s)
    nbb = s // tr
    with_u = mod3_u is not None
    if not with_u:
        mod3_u, idx_sc, idx_sh = mod3, idx_gate, idx_gate
    row = pl.BlockSpec((tr, d), lambda i: (i, 0))
    vec = pl.BlockSpec((1, d), lambda i: (0, 0))
    out_specs = [row, row] if with_u else [row]
    out_shape = [jax.ShapeDtypeStruct((t, d), F32)]
    if with_u:
        out_shape.append(jax.ShapeDtypeStruct((t, d), BF16))
    outs = pl.pallas_call(
        functools.partial(_res_ln_kernel, alpha=alpha, with_u=with_u),
        grid=(t // tr,),
        in_specs=[row, row, _mod_spec(d, nbb, idx_gate), vec, vec,
                  _mod_spec(d, nbb, idx_sc), _mod_spec(d, nbb, idx_sh)],
        out_specs=out_specs,
        out_shape=out_shape,
        compiler_params=_cparams(("parallel",)),
        name="res_ln",
    )(h, y, mod3, g.reshape(1, d), b.reshape(1, d), mod3_u, mod3_u)
    return outs if with_u else outs[0]


def _rms_cat_kernel(a_ref, b_ref, ga_ref, gb_ref, o_ref, *, wa):
    def rms(x, g):
        ms = jnp.mean(x * x, axis=-1, keepdims=True)
        return (x * lax.rsqrt(ms + LN_EPS) * g).astype(BF16)
    o_ref[:, :wa] = rms(a_ref[...], ga_ref[...])
    o_ref[:, wa:] = rms(b_ref[...], gb_ref[...])


def _rms_cat(oa, ob, ga, gb):
    t, wa = oa.shape
    wb = ob.shape[1]
    tr = min(512, t)
    return pl.pallas_call(
        functools.partial(_rms_cat_kernel, wa=wa),
        grid=(t // tr,),
        in_specs=[pl.BlockSpec((tr, wa), lambda i: (i, 0)), pl.BlockSpec((tr, wb), lambda i: (i, 0)),
                  pl.BlockSpec((1, wa), lambda i: (0, 0)), pl.BlockSpec((1, wb), lambda i: (0, 0))],
        out_specs=pl.BlockSpec((tr, wa + wb), lambda i: (i, 0)),
        out_shape=jax.ShapeDtypeStruct((t, wa + wb), BF16),
        compiler_params=_cparams(("parallel",)),
        name="rms_cat",
    )(oa, ob, ga.reshape(1, wa), gb.reshape(1, wb))


def _ikprep_kernel(t_ref, g_ref, b_ref, ke_ref, ko_ref):
    x = t_ref[...]
    lane = lax.broadcasted_iota(jnp.int32, x.shape, 1)
    is_k = lane < D_IDX
    mu = jnp.sum(jnp.where(is_k, x, 0.0), axis=-1, keepdims=True) * (1.0 / D_IDX)
    xc = jnp.where(is_k, x - mu, 0.0)
    var = jnp.sum(xc * xc, axis=-1, keepdims=True) * (1.0 / D_IDX)
    y = xc * lax.rsqrt(var + LN_EPS) * g_ref[...] + b_ref[...]
    ke_ref[...] = y.astype(BF16)
    ko_ref[...] = pltpu.roll(y, D_IDX, axis=1).astype(BF16)


def _ikprep(tail, g, b):
    t = tail.shape[0]
    tr = min(512, t)
    pad = lambda v: jnp.pad(v, (0, LANES - D_IDX)).reshape(1, LANES)
    row = pl.BlockSpec((tr, LANES), lambda i: (i, 0))
    vec = pl.BlockSpec((1, LANES), lambda i: (0, 0))
    return pl.pallas_call(
        _ikprep_kernel,
        grid=(t // tr,),
        in_specs=[row, vec, vec],
        out_specs=[row, row],
        out_shape=[jax.ShapeDtypeStruct((t, LANES), BF16)] * 2,
        compiler_params=_cparams(("parallel",)),
        name="ikprep",
    )(tail, pad(g), pad(b))


MM_TK_MAX = 4096


def _mm_kernel(a_ref, w_ref, o_ref, *, act, head_major, nk):
    r = jnp.dot(a_ref[...], w_ref[...], preferred_element_type=F32)
    if nk > 1:
        kk = pl.program_id(2)

        @pl.when(kk == 0)
        def _():
            o_ref[...] = r

        @pl.when(kk > 0)
        def _():
            o_ref[...] += r
        return
    if act == "relu2":
        r = jnp.square(jnp.maximum(r, 0.0))
    if head_major:
        for cblk in range(o_ref.shape[0]):
            o_ref[cblk] = r[:, cblk * LANES:(cblk + 1) * LANES].astype(o_ref.dtype)
    else:
        o_ref[...] = r.astype(o_ref.dtype)


def _mm_tiles(m, k, n):
    tm = min(1024, m)
    tn = 512 if n % 512 == 0 else LANES
    tk = k
    if k > MM_TK_MAX:
        tk = MM_TK_MAX
        tn = 1024 if n % 1024 == 0 else tn
    return tm, min(tn, n), tk


def _matmul(a, w, out_dtype, act=None, head_major=False, name="matmul"):
    m, k = a.shape
    n = w.shape[1]
    tm, tn, tk = _mm_tiles(m, k, n)
    nk = k // tk
    assert m % tm == 0 and n % tn == 0 and k % tk == 0
    assert nk == 1 or (act is None and not head_major and out_dtype == F32)
    if head_major:
        out_spec = pl.BlockSpec((tn // LANES, tm, LANES), lambda i, j, kk: (j, i, 0))
        out_shape = jax.ShapeDtypeStruct((n // LANES, m, LANES), out_dtype)
    else:
        out_spec = pl.BlockSpec((tm, tn), lambda i, j, kk: (i, j))
        out_shape = jax.ShapeDtypeStruct((m, n), out_dtype)
    return pl.pallas_call(
        functools.partial(_mm_kernel, act=act, head_major=head_major, nk=nk),
        grid=(m // tm, n // tn, nk),
        in_specs=[pl.BlockSpec((tm, tk), lambda i, j, kk: (i, kk)),
                  pl.BlockSpec((tk, tn), lambda i, j, kk: (kk, j))],
        out_specs=out_spec,
        out_shape=out_shape,
        compiler_params=_cparams(("parallel", "arbitrary", "arbitrary")),
        name=name,
    )(a, w)


IDX_TQ = 128
IDX_LC = 512


def _indexer_kernel(iq_ref, ke_ref, ko_ref, tail_ref, o_ref, zs_ref, keys_ref, wb_ref,
                    *, topk, s, w_scale):
    tq, lc = IDX_TQ, IDX_LC
    npair = H_IDX // 2
    qi = pl.program_id(1)
    t0 = qi * tq

    tl = tail_ref[...]
    for h in range(H_IDX):
        col = tl[:, D_IDX + h:D_IDX + h + 1] * w_scale
        wb_ref[h] = jnp.broadcast_to(col, (tq, LANES))

    a = iq_ref[...].reshape(npair * tq, LANES)
    nchunks = (t0 + tq + lc - 1) // lc

    def chunk_body(c, carry):
        c0 = pl.multiple_of(c * lc, lc)
        zs_ref[0] = _dot_nt(a, ke_ref[pl.ds(c0, lc), :])
        zs_ref[1] = _dot_nt(a, ko_ref[pl.ds(c0, lc), :])
        for r in range(tq // SUBLANES):
            rows = slice(r * SUBLANES, (r + 1) * SUBLANES)
            acc = jnp.zeros((SUBLANES, lc), F32)
            for p in range(npair):
                for par in range(2):
                    z = zs_ref[par, p * tq + r * SUBLANES:p * tq + (r + 1) * SUBLANES, :]
                    w = jnp.tile(wb_ref[2 * p + par, rows, :], (1, lc // LANES))
                    acc = acc + jnp.maximum(z, 0.0) * w
            t_idx = t0 + r * SUBLANES + lax.broadcasted_iota(jnp.int32, (SUBLANES, lc), 0)
            s_idx = c0 + lax.broadcasted_iota(jnp.int32, (SUBLANES, lc), 1)
            bits = pltpu.bitcast(acc, jnp.int32)
            key = bits ^ ((bits >> 31) & jnp.int32(0x7FFFFFFF))
            keys_ref[rows, pl.ds(c0, lc)] = jnp.where(s_idx <= t_idx, key, jnp.int32(INT_MIN))
        return carry

    lax.fori_loop(0, nchunks, chunk_body, 0)

    nl = nchunks * (lc // LANES)

    def bit_cond(carry):
        i, _, n_ge = carry
        return jnp.logical_and(i < 32, jnp.max(jnp.abs(n_ge - float(topk))) > 0.0)

    def bit_body(carry):
        i, ans, n_ge = carry
        cand = ans + lax.shift_left(jnp.int32(1), jnp.int32(31) - i)

        def cnt_body(c, cnt):
            c0 = pl.multiple_of(c * lc, lc)
            for j in range(lc // LANES):
                k = keys_ref[:, pl.ds(c0 + j * LANES, LANES)]
                cnt = cnt + jnp.where(k >= cand, 1.0, 0.0)
            return cnt

        cnt = lax.fori_loop(0, nchunks, cnt_body, jnp.zeros((tq, LANES), F32))
        tot = jnp.broadcast_to(jnp.sum(cnt, axis=1, keepdims=True), (tq, LANES))
        take = tot >= float(topk)
        return i + 1, jnp.where(take, cand, ans), jnp.where(take, tot, n_ge)

    n_all = jnp.broadcast_to((nl * LANES).astype(F32), (tq, LANES))
    _, ans, _ = lax.while_loop(
        bit_cond, bit_body, (jnp.int32(0), jnp.full((tq, LANES), INT_MIN, jnp.int32), n_all))
    thr = jnp.maximum(ans, jnp.int32(INT_MIN + 1))

    def sel_body(c, carry):
        cs = pl.multiple_of(c * LANES, LANES)
        k = keys_ref[:, pl.ds(cs, LANES)]
        o_ref[:, pl.ds(cs, LANES)] = jnp.where(k >= thr, 0.0, NEG_BIG).astype(BF16)
        return carry

    def fill_body(c, carry):
        cs = pl.multiple_of(c * LANES, LANES)
        o_ref[:, pl.ds(cs, LANES)] = jnp.full((tq, LANES), NEG_BIG, BF16)
        return carry

    lax.fori_loop(0, nl, sel_body, 0)
    lax.fori_loop(nl, s // LANES, fill_body, 0)


def _indexer(p_main, ke, ko, tail, b, s, topk):
    tq = IDX_TQ
    nq = s // tq
    npair = H_IDX // 2
    w_scale = (H_IDX ** -0.5) * (D_IDX ** -0.5)
    return pl.pallas_call(
        functools.partial(_indexer_kernel, topk=topk, s=s, w_scale=w_scale),
        grid=(b, nq),
        in_specs=[pl.BlockSpec((npair, tq, LANES), lambda bi, qi: (0, bi * nq + qi, 0)),
                  pl.BlockSpec((s, LANES), lambda bi, qi: (bi, 0)),
                  pl.BlockSpec((s, LANES), lambda bi, qi: (bi, 0)),
                  pl.BlockSpec((tq, LANES), lambda bi, qi: (bi * nq + qi, 0))],
        out_specs=pl.BlockSpec((tq, s), lambda bi, qi: (bi * nq + qi, 0)),
        out_shape=jax.ShapeDtypeStruct((b * s, s), BF16),
        scratch_shapes=[pltpu.VMEM((2, npair * tq, IDX_LC), F32),
                        pltpu.VMEM((tq, s), jnp.int32),
                        pltpu.VMEM((H_IDX, tq, LANES), F32)],
        compiler_params=_cparams(("parallel", "arbitrary")),
        name="indexer_topk",
    )(p_main, ke, ko, tail)


SP_TA = 256


def _sparse_kernel(qt_ref, kt_ref, q_ref, k_ref, v_ref, mb_ref, t0_ref, t1_ref, o_ref,
                   m_ref, l_ref, acc_ref, *, hs, g, ta):
    pair = pl.program_id(1)
    qi = qt_ref[pair]
    kj = kt_ref[pair]
    r = hs // g
    rows = r * ta

    @pl.when(kj == 0)
    def _():
        m_ref[...] = jnp.full(m_ref.shape, NEG_BIG, F32)
        l_ref[...] = jnp.zeros(l_ref.shape, F32)
        acc_ref[...] = jnp.zeros(acc_ref.shape, F32)

    def step(toep_ref):
        mb = mb_ref[...].astype(F32)
        for gi in range(g):
            sl = slice(gi * rows, (gi + 1) * rows)
            qg = q_ref[gi * r:(gi + 1) * r].reshape(rows, HEAD_DIM)
            sc = _dot_nt(qg, k_ref[gi]).reshape(r, ta, ta) + mb[None]
            if toep_ref is not None:
                sc = sc + toep_ref[gi * r:(gi + 1) * r]
            sc = sc.reshape(rows, ta)
            m_prev = m_ref[sl]
            m_next = jnp.maximum(m_prev, jnp.max(sc, axis=1, keepdims=True))
            p = jnp.exp(sc - jnp.tile(m_next, (1, ta // LANES)))
            alpha = jnp.exp(m_prev - m_next)
            l_ref[sl] = alpha * l_ref[sl] + jnp.sum(p, axis=1, keepdims=True)
            m_ref[sl] = m_next
            acc_ref[sl] = acc_ref[sl] * alpha + jnp.dot(p.astype(BF16), v_ref[gi],
                                                         preferred_element_type=F32)

    @pl.when(kj < qi - 1)
    def _():
        step(None)

    @pl.when(kj == qi - 1)
    def _():
        step(t1_ref)

    @pl.when(kj == qi)
    def _():
        step(t0_ref)
        for h in range(hs):
            hsl = slice(h * ta, (h + 1) * ta)
            o_ref[:, h * HEAD_DIM:(h + 1) * HEAD_DIM] = acc_ref[hsl] / l_ref[hsl]


def _sparse_attention(p_main, maskb, toep0, toep1, b, s, hs, g, off_q, off_k, off_v):
    ta = min(SP_TA, s)
    nq = s // ta
    assert off_q % hs == 0 and off_k % g == 0 and off_v % g == 0
    pairs = [(qi, kj) for qi in range(nq) for kj in range(qi + 1)]
    qt = jnp.asarray([p[0] for p in pairs], jnp.int32)
    kt = jnp.asarray([p[1] for p in pairs], jnp.int32)
    grid_spec = pltpu.PrefetchScalarGridSpec(
        num_scalar_prefetch=2,
        grid=(b, len(pairs)),
        in_specs=[pl.BlockSpec((hs, ta, LANES), lambda bi, p, qt, kt: (off_q // hs, bi * nq + qt[p], 0)),
                  pl.BlockSpec((g, ta, LANES), lambda bi, p, qt, kt: (off_k // g, bi * nq + kt[p], 0)),
                  pl.BlockSpec((g, ta, LANES), lambda bi, p, qt, kt: (off_v // g, bi * nq + kt[p], 0)),
                  pl.BlockSpec((ta, ta), lambda bi, p, qt, kt: (bi * nq + qt[p], kt[p])),
                  pl.BlockSpec((hs, ta, ta), lambda bi, p, qt, kt: (0, 0, 0)),
                  pl.BlockSpec((hs, ta, ta), lambda bi, p, qt, kt: (0, 0, 0))],
        out_specs=pl.BlockSpec((ta, hs * HEAD_DIM), lambda bi, p, qt, kt: (bi * nq + qt[p], 0)),
        scratch_shapes=[pltpu.VMEM((hs * ta, LANES), F32)] * 3,
    )
    return pl.pallas_call(
        functools.partial(_sparse_kernel, hs=hs, g=g, ta=ta),
        grid_spec=grid_spec,
        out_shape=jax.ShapeDtypeStruct((b * s, hs * HEAD_DIM), F32),
        compiler_params=_cparams(("parallel", "arbitrary")),
        name="sparse_attn",
    )(qt, kt, p_main, p_main, p_main, maskb, toep0, toep1)


def _t5_bucket(dist):
    n = jnp.maximum(dist, 0)
    max_exact = N_BUCKETS // 2
    nf = jnp.maximum(n, 1).astype(F32)
    large = max_exact + (jnp.log(nf / max_exact) / math.log(MAX_DISTANCE / max_exact)
                         * (N_BUCKETS - max_exact)).astype(jnp.int32)
    large = jnp.minimum(large, N_BUCKETS - 1)
    return jnp.where(n < max_exact, n, large)


def _toeplitz_bias(rel_bias, ta):
    assert ta >= MAX_DISTANCE
    nh = rel_bias.shape[1]
    rr = jnp.arange(ta, dtype=jnp.int32)[:, None]
    cc = jnp.arange(ta, dtype=jnp.int32)[None, :]
    far = rel_bias[_t5_bucket(jnp.int32(2 * ta))]
    shifted = (rel_bias - far[None, :]).T

    def table(dist):
        onehot = (_t5_bucket(dist).reshape(1, ta * ta)
                  == jnp.arange(N_BUCKETS, dtype=jnp.int32)[:, None]).astype(F32)
        return jnp.dot(shifted, onehot, precision=lax.Precision.HIGHEST).reshape(nh, ta, ta)

    return table(rr - cc), table(rr - cc + ta)


SB_SUB = 128
SB_M = 4
SB_GS = 4
SB_CH = 64


def _sb_kernel(q_ref, k_ref, v_ref, tri_ref, o_ref, z_ref, lb_ref, hl_ref, both_ref, a_ref,
               acc_ref, run_ref, *, m, gs):
    n_sub = SB_SUB
    ch = SB_CH
    rows = gs * m * n_sub
    nch = rows // ch
    base = pl.program_id(2) * m

    def key_start(r, n):
        return pl.multiple_of(jnp.maximum(base + r - n, 0) * n_sub, n_sub)

    def strict_mask(r0):
        row = (r0 % n_sub) + lax.broadcasted_iota(jnp.int32, (ch, n_sub), 0)
        col = lax.broadcasted_iota(jnp.int32, (ch, n_sub), 1)
        return col < row

    def sweep(n, first):
        for hh in range(gs):
            for r in range(m):
                i = hh * m + r
                q = q_ref[hh, r * n_sub:(r + 1) * n_sub, :]
                z_ref[i * n_sub:(i + 1) * n_sub, :] = _dot_nt(q, k_ref[hh, pl.ds(key_start(r, n), n_sub), :])

        def split_body(c, carry):
            r0 = pl.multiple_of(c * ch, ch)
            z = z_ref[pl.ds(r0, ch), :]
            sp = jnp.log(1.0 + jnp.exp(-jnp.abs(z)))
            lb_ref[pl.ds(r0, ch), :] = jnp.minimum(z, 0.0) - sp
            log_keep = -jnp.maximum(z, 0.0) - sp
            if first:
                log_keep = jnp.where(strict_mask(r0), log_keep, 0.0)
            hl_ref[pl.ds(r0, ch), :] = log_keep.astype(BF16)
            return carry

        lax.fori_loop(0, nch, split_body, 0, unroll=2)

        both_ref[...] = jnp.dot(hl_ref[...], tri_ref[...], preferred_element_type=F32)

        def weight_body(c, mx):
            r0 = pl.multiple_of(c * ch, ch)
            both = both_ref[pl.ds(r0, ch), :]
            later = both[:, :n_sub]
            rowsum = both[:, n_sub:]
            if first:
                run = rowsum
            else:
                valid = base + (r0 // n_sub) % m - n >= 0
                prev = run_ref[pl.ds(r0, ch), :]
                later = later + prev
                run = prev + jnp.where(valid, rowsum, 0.0)
            a = jnp.exp(lb_ref[pl.ds(r0, ch), :] + later)
            if first:
                a = jnp.where(strict_mask(r0), a, 0.0)
            else:
                a = jnp.where(valid, a, 0.0)
            a_ref[pl.ds(r0, ch), :] = a.astype(BF16)
            run_ref[pl.ds(r0, ch), :] = run
            return jnp.maximum(mx, run)

        mx = lax.fori_loop(0, nch, weight_body, jnp.full((ch, n_sub), -jnp.inf, F32), unroll=2)

        for hh in range(gs):
            for r in range(m):
                i = hh * m + r
                sl = slice(i * n_sub, (i + 1) * n_sub)
                pv = jnp.dot(a_ref[sl, :], v_ref[hh, pl.ds(key_start(r, n), n_sub), :],
                             preferred_element_type=F32)
                if first:
                    acc_ref[sl, :] = pv
                else:
                    acc_ref[sl, :] = acc_ref[sl, :] + pv
        return jnp.max(mx)

    mx0 = sweep(0, True)

    def cond(carry):
        n, mx = carry
        return jnp.logical_and(n <= base + m - 1, mx > SB_EXIT)

    def body(carry):
        n, _ = carry
        return n + 1, sweep(n, False)

    lax.while_loop(cond, body, (jnp.int32(1), mx0))

    for hh in range(gs):
        for r in range(m):
            i = hh * m + r
            o_ref[r * n_sub:(r + 1) * n_sub, hh * HEAD_DIM:(hh + 1) * HEAD_DIM] = (
                acc_ref[i * n_sub:(i + 1) * n_sub, :])


def _stick_breaking(p_main, b, s, hb, off_q, off_k, off_v):
    gs = SB_GS
    m = SB_M
    tq = SB_SUB * m
    assert s % tq == 0 and hb % gs == 0
    assert off_q % gs == 0 and off_k % gs == 0 and off_v % gs == 0
    nq = s // tq
    rows = gs * m * SB_SUB
    idx = jnp.arange(SB_SUB)
    tri = jnp.concatenate([(idx[:, None] > idx[None, :]).astype(BF16),
                           jnp.ones((SB_SUB, SB_SUB), BF16)], axis=1)
    return pl.pallas_call(
        functools.partial(_sb_kernel, m=m, gs=gs),
        grid=(b, hb // gs, nq),
        in_specs=[pl.BlockSpec((gs, tq, LANES), lambda bi, gi, qi: (off_q // gs + gi, bi * nq + qi, 0)),
                  pl.BlockSpec((gs, s, LANES), lambda bi, gi, qi: (off_k // gs + gi, bi, 0)),
                  pl.BlockSpec((gs, s, LANES), lambda bi, gi, qi: (off_v // gs + gi, bi, 0)),
                  pl.BlockSpec((SB_SUB, 2 * SB_SUB), lambda bi, gi, qi: (0, 0))],
        out_specs=pl.BlockSpec((tq, gs * HEAD_DIM), lambda bi, gi, qi: (bi * nq + qi, gi)),
        out_shape=jax.ShapeDtypeStruct((b * s, hb * HEAD_DIM), F32),
        scratch_shapes=[pltpu.VMEM((rows, LANES), F32),
                        pltpu.VMEM((rows, LANES), F32),
                        pltpu.VMEM((rows, LANES), BF16),
                        pltpu.VMEM((rows, 2 * LANES), F32),
                        pltpu.VMEM((rows, LANES), BF16),
                        pltpu.VMEM((rows, LANES), F32),
                        pltpu.VMEM((rows, LANES), F32)],
        compiler_params=_cparams(("parallel", "parallel", "arbitrary")),
        name="stick_breaking",
    )(p_main, p_main, p_main, tri)


def kernel(x, c, in_ln_g, in_ln_b, rel_bias, w_ada, b_ada, w_in, idx_kn_g, idx_kn_b,
           gn_sparse_g, gn_sb_g, w_out, ln1_g, ln1_b, w_up, w_down, ln2_g, ln2_b):
    b, s, d = x.shape
    depth = w_ada.shape[0]
    t = b * s
    n_heads = d // HEAD_DIM
    hs = n_heads // 2
    g = hs // 4
    hb = n_heads - hs
    ws, wkv, wsb = hs * HEAD_DIM, g * HEAD_DIM, hb * HEAD_DIM
    wiq = H_IDX * D_IDX
    topk = min(TOPK_MAX, s // 4)
    alpha = (2.0 * depth) ** 0.25
    qscale = HEAD_DIM ** -0.5

    off_aq = wiq // LANES
    off_ak = off_aq + hs
    off_av = off_ak + g
    off_bq = off_av + g
    off_bk = off_bq + hb
    off_bv = off_bk + hb

    mods = [_ada(c, w_ada[l], b_ada[l]).reshape(b * 6, 1, d) for l in range(depth)]
    h, u = _ln_mod(x.reshape(t, d), in_ln_g, in_ln_b, mods[0], s, 1, 0)
    toep0, toep1 = _toeplitz_bias(rel_bias, min(SP_TA, s))
    for l in range(depth):
        mod3 = mods[l]
        o = 0
        cols = {}
        for name, size in (("aq", ws), ("ak", wkv), ("av", wkv), ("bq", wsb), ("bk", wsb),
                           ("bv", wsb), ("iq", wiq), ("ik", D_IDX), ("iw", H_IDX)):
            cols[name] = w_in[l][:, o:o + size]
            o += size
        w_main = jnp.concatenate([cols["iq"], cols["aq"] * qscale, cols["ak"], cols["av"],
                                  cols["bq"] * qscale, cols["bk"], cols["bv"]], axis=1).astype(BF16)
        w_tail = jnp.pad(jnp.concatenate([cols["ik"], cols["iw"]], axis=1),
                         ((0, 0), (0, LANES - D_IDX - H_IDX))).astype(BF16)
        p_main = _matmul(u, w_main, BF16, head_major=True, name="in_proj")
        tail = _matmul(u, w_tail, F32, name="in_proj_tail")
        ke, ko = _ikprep(tail, idx_kn_g[l], idx_kn_b[l])
        maskb = _indexer(p_main, ke, ko, tail, b, s, topk)
        o_a = _sparse_attention(p_main, maskb, toep0, toep1, b, s, hs, g, off_aq, off_ak, off_av)
        o_b = _stick_breaking(p_main, b, s, hb, off_bq, off_bk, off_bv)
        xn = _rms_cat(o_a, o_b, gn_sparse_g[l], gn_sb_g[l])
        mixed = _matmul(xn, w_out[l].astype(BF16), F32, name="out_proj")
        h, u = _res_ln(h, mixed, mod3, s, 2, ln1_g[l], ln1_b[l], alpha,
                       mod3_u=mod3, idx_sc=4, idx_sh=3)
        hmid = _matmul(u, w_up[l].astype(BF16), BF16, act="relu2", name="mlp_up")
        y = _matmul(hmid, w_down[l].astype(BF16), F32, name="mlp_down")
        if l + 1 < depth:
            h, u = _res_ln(h, y, mod3, s, 5, ln2_g[l], ln2_b[l], alpha,
                           mod3_u=mods[l + 1], idx_sc=1, idx_sh=0)
        else:
            h = _res_ln(h, y, mod3, s, 5, ln2_g[l], ln2_b[l], alpha)
    return h.reshape(b, s, d)
```

```python
import functools
import math

import jax
import jax.numpy as jnp
from jax import lax
from jax.experimental import pallas as pl
from jax.experimental.pallas import tpu as pltpu

HEAD_DIM = 128
H_IDX = 32
D_IDX = 64
TOPK_MAX = 256
N_BUCKETS = 32
MAX_DISTANCE = 128
LN_EPS = 1e-5

LANES = 128
SUBLANES = 8
VMEM_LIMIT = 56 * 1024 * 1024
NEG_BIG = -1e30
INT_MIN = -2147483648
SB_EXIT = -110.0

F32 = jnp.float32
BF16 = jnp.bfloat16


def _cparams(sem):
    return pltpu.CompilerParams(dimension_semantics=sem, vmem_limit_bytes=VMEM_LIMIT)


def _dot_nt(a, b):
    return lax.dot_general(a, b, (((1,), (1,)), ((), ())), preferred_element_type=F32)


def _ada_kernel(c_ref, w_ref, b_ref, o_ref, cs_ref, *, nb, d, tn):
    @pl.when(pl.program_id(0) == 0)
    def _():
        cv = c_ref[...]
        cs_ref[...] = cv * jax.nn.sigmoid(cv)

    nj = tn // LANES

    def body(k, accs):
        k8 = pl.multiple_of(k * SUBLANES, SUBLANES)
        out = list(accs)
        cs = [cs_ref[b, pl.ds(k8, SUBLANES), :] for b in range(nb)]
        for j in range(nj):
            wk = w_ref[pl.ds(k8, SUBLANES), j * LANES:(j + 1) * LANES]
            for b in range(nb):
                out[b * nj + j] = out[b * nj + j] + wk * cs[b]
        return tuple(out)

    init = tuple(jnp.zeros((SUBLANES, LANES), F32) for _ in range(nb * nj))
    accs = lax.fori_loop(0, d // SUBLANES, body, init, unroll=4)
    for b in range(nb):
        row = jnp.concatenate(
            [jnp.sum(accs[b * nj + j], axis=0, keepdims=True) for j in range(nj)], axis=1)
        o_ref[b:b + 1, :] = row + b_ref[...]


def _ada(c, w, bias):
    nb, d = c.shape
    n = w.shape[1]
    tn = 512 if n % 512 == 0 else LANES
    cb = jnp.broadcast_to(c[:, :, None], (nb, d, LANES))
    return pl.pallas_call(
        functools.partial(_ada_kernel, nb=nb, d=d, tn=tn),
        grid=(n // tn,),
        in_specs=[pl.BlockSpec((nb, d, LANES), lambda j: (0, 0, 0)),
                  pl.BlockSpec((d, tn), lambda j: (0, j)),
                  pl.BlockSpec((1, tn), lambda j: (0, j))],
        out_specs=pl.BlockSpec((nb, tn), lambda j: (0, j)),
        out_shape=jax.ShapeDtypeStruct((nb, n), F32),
        scratch_shapes=[pltpu.VMEM((nb, d, LANES), F32)],
        compiler_params=_cparams(("arbitrary",)),
        name="ada_mod",
    )(cb, w, bias.reshape(1, n))


def _layer_norm_rows(x, g, b):
    mu = jnp.mean(x, axis=-1, keepdims=True)
    xc = x - mu
    var = jnp.mean(xc * xc, axis=-1, keepdims=True)
    return xc * lax.rsqrt(var + LN_EPS) * g + b


def _ln_mod_kernel(x_ref, g_ref, b_ref, sc_ref, sh_ref, h_ref, u_ref):
    h = _layer_norm_rows(x_ref[...], g_ref[...], b_ref[...])
    h_ref[...] = h
    u_ref[...] = (h * (1.0 + sc_ref[0]) + sh_ref[0]).astype(BF16)


def _mod_spec(d, rows_per_batch_blocks, idx):
    return pl.BlockSpec((1, 1, d), lambda i: ((i // rows_per_batch_blocks) * 6 + idx, 0, 0))


def _ln_mod(x2, g, b, mod3, s, idx_sc, idx_sh):
    t, d = x2.shape
    tr = min(256, s)
    nbb = s // tr
    row = pl.BlockSpec((tr, d), lambda i: (i, 0))
    vec = pl.BlockSpec((1, d), lambda i: (0, 0))
    return pl.pallas_call(
        _ln_mod_kernel,
        grid=(t // tr,),
        in_specs=[row, vec, vec, _mod_spec(d, nbb, idx_sc), _mod_spec(d, nbb, idx_sh)],
        out_specs=[row, row],
        out_shape=[jax.ShapeDtypeStruct((t, d), F32), jax.ShapeDtypeStruct((t, d), BF16)],
        compiler_params=_cparams(("parallel",)),
        name="ln_mod",
    )(x2, g.reshape(1, d), b.reshape(1, d), mod3, mod3)


def _res_ln_kernel(h_ref, y_ref, gate_ref, g_ref, b_ref, sc_ref, sh_ref, *out_refs, alpha, with_u):
    h = _layer_norm_rows(alpha * h_ref[...] + gate_ref[0] * y_ref[...], g_ref[...], b_ref[...])
    out_refs[0][...] = h
    if with_u:
        out_refs[1][...] = (h * (1.0 + sc_ref[0]) + sh_ref[0]).astype(BF16)


def _res_ln(h, y, mod3, s, idx_gate, g, b, alpha, mod3_u=None, idx_sc=None, idx_sh=None):
    t, d = h.shape
    tr = min(256, s)
    nbb = s // tr
    with_u = mod3_u is not None
    if not with_u:
        mod3_u, idx_sc, idx_sh = mod3, idx_gate, idx_gate
    row = pl.BlockSpec((tr, d), lambda i: (i, 0))
    vec = pl.BlockSpec((1, d), lambda i: (0, 0))
    out_specs = [row, row] if with_u else [row]
    out_shape = [jax.ShapeDtypeStruct((t, d), F32)]
    if with_u:
        out_shape.append(jax.ShapeDtypeStruct((t, d), BF16))
    outs = pl.pallas_call(
        functools.partial(_res_ln_kernel, alpha=alpha, with_u=with_u),
        grid=(t // tr,),
        in_specs=[row, row, _mod_spec(d, nbb, idx_gate), vec, vec,
                  _mod_spec(d, nbb, idx_sc), _mod_spec(d, nbb, idx_sh)],
        out_specs=out_specs,
        out_shape=out_shape,
        compiler_params=_cparams(("parallel",)),
        name="res_ln",
    )(h, y, mod3, g.reshape(1, d), b.reshape(1, d), mod3_u, mod3_u)
    return outs if with_u else outs[0]


def _rms_cat_kernel(a_ref, b_ref, ga_ref, gb_ref, o_ref, *, wa):
    def rms(x, g):
        ms = jnp.mean(x * x, axis=-1, keepdims=True)
        return (x * lax.rsqrt(ms + LN_EPS) * g).astype(BF16)
    o_ref[:, :wa] = rms(a_ref[...], ga_ref[...])
    o_ref[:, wa:] = rms(b_ref[...], gb_ref[...])


def _rms_cat(oa, ob, ga, gb):
    t, wa = oa.shape
    wb = ob.shape[1]
    tr = min(512, t)
    return pl.pallas_call(
        functools.partial(_rms_cat_kernel, wa=wa),
        grid=(t // tr,),
        in_specs=[pl.BlockSpec((tr, wa), lambda i: (i, 0)), pl.BlockSpec((tr, wb), lambda i: (i, 0)),
                  pl.BlockSpec((1, wa), lambda i: (0, 0)), pl.BlockSpec((1, wb), lambda i: (0, 0))],
        out_specs=pl.BlockSpec((tr, wa + wb), lambda i: (i, 0)),
        out_shape=jax.ShapeDtypeStruct((t, wa + wb), BF16),
        compiler_params=_cparams(("parallel",)),
        name="rms_cat",
    )(oa, ob, ga.reshape(1, wa), gb.reshape(1, wb))


def _ikprep_kernel(t_ref, g_ref, b_ref, ke_ref, ko_ref):
    x = t_ref[...]
    lane = lax.broadcasted_iota(jnp.int32, x.shape, 1)
    is_k = lane < D_IDX
    mu = jnp.sum(jnp.where(is_k, x, 0.0), axis=-1, keepdims=True) * (1.0 / D_IDX)
    xc = jnp.where(is_k, x - mu, 0.0)
    var = jnp.sum(xc * xc, axis=-1, keepdims=True) * (1.0 / D_IDX)
    y = xc * lax.rsqrt(var + LN_EPS) * g_ref[...] + b_ref[...]
    ke_ref[...] = y.astype(BF16)
    ko_ref[...] = pltpu.roll(y, D_IDX, axis=1).astype(BF16)


def _ikprep(tail, g, b):
    t = tail.shape[0]
    tr = min(512, t)
    pad = lambda v: jnp.pad(v, (0, LANES - D_IDX)).reshape(1, LANES)
    row = pl.BlockSpec((tr, LANES), lambda i: (i, 0))
    vec = pl.BlockSpec((1, LANES), lambda i: (0, 0))
    return pl.pallas_call(
        _ikprep_kernel,
        grid=(t // tr,),
        in_specs=[row, vec, vec],
        out_specs=[row, row],
        out_shape=[jax.ShapeDtypeStruct((t, LANES), BF16)] * 2,
        compiler_params=_cparams(("parallel",)),
        name="ikprep",
    )(tail, pad(g), pad(b))


MM_TK_MAX = 4096


def _mm_kernel(a_ref, w_ref, o_ref, *, act, head_major, nk):
    r = jnp.dot(a_ref[...], w_ref[...], preferred_element_type=F32)
    if nk > 1:
        kk = pl.program_id(2)

        @pl.when(kk == 0)
        def _():
            o_ref[...] = r

        @pl.when(kk > 0)
        def _():
            o_ref[...] += r
        return
    if act == "relu2":
        r = jnp.square(jnp.maximum(r, 0.0))
    if head_major:
        for cblk in range(o_ref.shape[0]):
            o_ref[cblk] = r[:, cblk * LANES:(cblk + 1) * LANES].astype(o_ref.dtype)
    else:
        o_ref[...] = r.astype(o_ref.dtype)


def _mm_tiles(m, k, n):
    tm = min(1024, m)
    tn = 512 if n % 512 == 0 else LANES
    tk = k
    if k > MM_TK_MAX:
        tk = MM_TK_MAX
        tn = 1024 if n % 1024 == 0 else tn
    return tm, min(tn, n), tk


def _matmul(a, w, out_dtype, act=None, head_major=False, name="matmul"):
    m, k = a.shape
    n = w.shape[1]
    tm, tn, tk = _mm_tiles(m, k, n)
    nk = k // tk
    assert m % tm == 0 and n % tn == 0 and k % tk == 0
    assert nk == 1 or (act is None and not head_major and out_dtype == F32)
    if head_major:
        out_spec = pl.BlockSpec((tn // LANES, tm, LANES), lambda i, j, kk: (j, i, 0))
        out_shape = jax.ShapeDtypeStruct((n // LANES, m, LANES), out_dtype)
    else:
        out_spec = pl.BlockSpec((tm, tn), lambda i, j, kk: (i, j))
        out_shape = jax.ShapeDtypeStruct((m, n), out_dtype)
    return pl.pallas_call(
        functools.partial(_mm_kernel, act=act, head_major=head_major, nk=nk),
        grid=(m // tm, n // tn, nk),
        in_specs=[pl.BlockSpec((tm, tk), lambda i, j, kk: (i, kk)),
                  pl.BlockSpec((tk, tn), lambda i, j, kk: (kk, j))],
        out_specs=out_spec,
        out_shape=out_shape,
        compiler_params=_cparams(("parallel", "arbitrary", "arbitrary")),
        name=name,
    )(a, w)


IDX_TQ = 128
IDX_LC = 512


def _indexer_kernel(iq_ref, ke_ref, ko_ref, tail_ref, o_ref, zs_ref, keys_ref, wb_ref,
                    *, topk, s, w_scale):
    tq, lc = IDX_TQ, IDX_LC
    npair = H_IDX // 2
    qi = pl.program_id(1)
    t0 = qi * tq

    tl = tail_ref[...]
    for h in range(H_IDX):
        col = tl[:, D_IDX + h:D_IDX + h + 1] * w_scale
        wb_ref[h] = jnp.broadcast_to(col, (tq, LANES))

    a = iq_ref[...].reshape(npair * tq, LANES)
    nchunks = (t0 + tq + lc - 1) // lc

    def chunk_body(c, carry):
        c0 = pl.multiple_of(c * lc, lc)
        zs_ref[0] = _dot_nt(a, ke_ref[pl.ds(c0, lc), :])
        zs_ref[1] = _dot_nt(a, ko_ref[pl.ds(c0, lc), :])
        for r in range(tq // SUBLANES):
            rows = slice(r * SUBLANES, (r + 1) * SUBLANES)
            acc = jnp.zeros((SUBLANES, lc), F32)
            for p in range(npair):
                for par in range(2):
                    z = zs_ref[par, p * tq + r * SUBLANES:p * tq + (r + 1) * SUBLANES, :]
                    w = jnp.tile(wb_ref[2 * p + par, rows, :], (1, lc // LANES))
                    acc = acc + jnp.maximum(z, 0.0) * w
            t_idx = t0 + r * SUBLANES + lax.broadcasted_iota(jnp.int32, (SUBLANES, lc), 0)
            s_idx = c0 + lax.broadcasted_iota(jnp.int32, (SUBLANES, lc), 1)
            bits = pltpu.bitcast(acc, jnp.int32)
            key = bits ^ ((bits >> 31) & jnp.int32(0x7FFFFFFF))
            keys_ref[rows, pl.ds(c0, lc)] = jnp.where(s_idx <= t_idx, key, jnp.int32(INT_MIN))
        return carry

    lax.fori_loop(0, nchunks, chunk_body, 0)

    nl = nchunks * (lc // LANES)

    def bit_cond(carry):
        i, _, n_ge = carry
        return jnp.logical_and(i < 32, jnp.max(jnp.abs(n_ge - float(topk))) > 0.0)

    def bit_body(carry):
        i, ans, n_ge = carry
        cand = ans + lax.shift_left(jnp.int32(1), jnp.int32(31) - i)

        def cnt_body(c, cnt):
            c0 = pl.multiple_of(c * lc, lc)
            for j in range(lc // LANES):
                k = keys_ref[:, pl.ds(c0 + j * LANES, LANES)]
                cnt = cnt + jnp.where(k >= cand, 1.0, 0.0)
            return cnt

        cnt = lax.fori_loop(0, nchunks, cnt_body, jnp.zeros((tq, LANES), F32))
        tot = jnp.broadcast_to(jnp.sum(cnt, axis=1, keepdims=True), (tq, LANES))
        take = tot >= float(topk)
        return i + 1, jnp.where(take, cand, ans), jnp.where(take, tot, n_ge)

    n_all = jnp.broadcast_to((nl * LANES).astype(F32), (tq, LANES))
    _, ans, _ = lax.while_loop(
        bit_cond, bit_body, (jnp.int32(0), jnp.full((tq, LANES), INT_MIN, jnp.int32), n_all))
    thr = jnp.maximum(ans, jnp.int32(INT_MIN + 1))

    def sel_body(c, carry):
        cs = pl.multiple_of(c * LANES, LANES)
        k = keys_ref[:, pl.ds(cs, LANES)]
        o_ref[:, pl.ds(cs, LANES)] = jnp.where(k >= thr, 0.0, NEG_BIG).astype(BF16)
        return carry

    def fill_body(c, carry):
        cs = pl.multiple_of(c * LANES, LANES)
        o_ref[:, pl.ds(cs, LANES)] = jnp.full((tq, LANES), NEG_BIG, BF16)
        return carry

    lax.fori_loop(0, nl, sel_body, 0)
    lax.fori_loop(nl, s // LANES, fill_body, 0)


def _indexer(p_main, ke, ko, tail, b, s, topk):
    tq = IDX_TQ
    nq = s // tq
    npair = H_IDX // 2
    w_scale = (H_IDX ** -0.5) * (D_IDX ** -0.5)
    return pl.pallas_call(
        functools.partial(_indexer_kernel, topk=topk, s=s, w_scale=w_scale),
        grid=(b, nq),
        in_specs=[pl.BlockSpec((npair, tq, LANES), lambda bi, qi: (0, bi * nq + qi, 0)),
                  pl.BlockSpec((s, LANES), lambda bi, qi: (bi, 0)),
                  pl.BlockSpec((s, LANES), lambda bi, qi: (bi, 0)),
                  pl.BlockSpec((tq, LANES), lambda bi, qi: (bi * nq + qi, 0))],
        out_specs=pl.BlockSpec((tq, s), lambda bi, qi: (bi * nq + qi, 0)),
        out_shape=jax.ShapeDtypeStruct((b * s, s), BF16),
        scratch_shapes=[pltpu.VMEM((2, npair * tq, IDX_LC), F32),
                        pltpu.VMEM((tq, s), jnp.int32),
                        pltpu.VMEM((H_IDX, tq, LANES), F32)],
        compiler_params=_cparams(("parallel", "arbitrary")),
        name="indexer_topk",
    )(p_main, ke, ko, tail)


SP_TA = 256


def _sparse_kernel(qt_ref, kt_ref, q_ref, k_ref, v_ref, mb_ref, t0_ref, t1_ref, o_ref,
                   m_ref, l_ref, acc_ref, *, hs, g, ta):
    pair = pl.program_id(1)
    qi = qt_ref[pair]
    kj = kt_ref[pair]
    r = hs // g
    rows = r * ta

    @pl.when(kj == 0)
    def _():
        m_ref[...] = jnp.full(m_ref.shape, NEG_BIG, F32)
        l_ref[...] = jnp.zeros(l_ref.shape, F32)
        acc_ref[...] = jnp.zeros(acc_ref.shape, F32)

    def step(toep_ref):
        mb = mb_ref[...].astype(F32)
        for gi in range(g):
            sl = slice(gi * rows, (gi + 1) * rows)
            qg = q_ref[gi * r:(gi + 1) * r].reshape(rows, HEAD_DIM)
            sc = _dot_nt(qg, k_ref[gi]).reshape(r, ta, ta) + mb[None]
            if toep_ref is not None:
                sc = sc + toep_ref[gi * r:(gi + 1) * r]
            sc = sc.reshape(rows, ta)
            m_prev = m_ref[sl]
            m_next = jnp.maximum(m_prev, jnp.max(sc, axis=1, keepdims=True))
            p = jnp.exp(sc - jnp.tile(m_next, (1, ta // LANES)))
            alpha = jnp.exp(m_prev - m_next)
            l_ref[sl] = alpha * l_ref[sl] + jnp.sum(p, axis=1, keepdims=True)
            m_ref[sl] = m_next
            acc_ref[sl] = acc_ref[sl] * alpha + jnp.dot(p.astype(BF16), v_ref[gi],
                                                         preferred_element_type=F32)

    @pl.when(kj < qi - 1)
    def _():
        step(None)

    @pl.when(kj == qi - 1)
    def _():
        step(t1_ref)

    @pl.when(kj == qi)
    def _():
        step(t0_ref)
        for h in range(hs):
            hsl = slice(h * ta, (h + 1) * ta)
            o_ref[:, h * HEAD_DIM:(h + 1) * HEAD_DIM] = acc_ref[hsl] / l_ref[hsl]


def _sparse_attention(p_main, maskb, toep0, toep1, b, s, hs, g, off_q, off_k, off_v):
    ta = min(SP_TA, s)
    nq = s // ta
    assert off_q % hs == 0 and off_k % g == 0 and off_v % g == 0
    pairs = [(qi, kj) for qi in range(nq) for kj in range(qi + 1)]
    qt = jnp.asarray([p[0] for p in pairs], jnp.int32)
    kt = jnp.asarray([p[1] for p in pairs], jnp.int32)
    grid_spec = pltpu.PrefetchScalarGridSpec(
        num_scalar_prefetch=2,
        grid=(b, len(pairs)),
        in_specs=[pl.BlockSpec((hs, ta, LANES), lambda bi, p, qt, kt: (off_q // hs, bi * nq + qt[p], 0)),
                  pl.BlockSpec((g, ta, LANES), lambda bi, p, qt, kt: (off_k // g, bi * nq + kt[p], 0)),
                  pl.BlockSpec((g, ta, LANES), lambda bi, p, qt, kt: (off_v // g, bi * nq + kt[p], 0)),
                  pl.BlockSpec((ta, ta), lambda bi, p, qt, kt: (bi * nq + qt[p], kt[p])),
                  pl.BlockSpec((hs, ta, ta), lambda bi, p, qt, kt: (0, 0, 0)),
                  pl.BlockSpec((hs, ta, ta), lambda bi, p, qt, kt: (0, 0, 0))],
        out_specs=pl.BlockSpec((ta, hs * HEAD_DIM), lambda bi, p, qt, kt: (bi * nq + qt[p], 0)),
        scratch_shapes=[pltpu.VMEM((hs * ta, LANES), F32)] * 3,
    )
    return pl.pallas_call(
        functools.partial(_sparse_kernel, hs=hs, g=g, ta=ta),
        grid_spec=grid_spec,
        out_shape=jax.ShapeDtypeStruct((b * s, hs * HEAD_DIM), F32),
        compiler_params=_cparams(("parallel", "arbitrary")),
        name="sparse_attn",
    )(qt, kt, p_main, p_main, p_main, maskb, toep0, toep1)


def _t5_bucket(dist):
    n = jnp.maximum(dist, 0)
    max_exact = N_BUCKETS // 2
    nf = jnp.maximum(n, 1).astype(F32)
    large = max_exact + (jnp.log(nf / max_exact) / math.log(MAX_DISTANCE / max_exact)
                         * (N_BUCKETS - max_exact)).astype(jnp.int32)
    large = jnp.minimum(large, N_BUCKETS - 1)
    return jnp.where(n < max_exact, n, large)


def _toeplitz_bias(rel_bias, ta):
    assert ta >= MAX_DISTANCE
    nh = rel_bias.shape[1]
    rr = jnp.arange(ta, dtype=jnp.int32)[:, None]
    cc = jnp.arange(ta, dtype=jnp.int32)[None, :]
    far = rel_bias[_t5_bucket(jnp.int32(2 * ta))]
    shifted = (rel_bias - far[None, :]).T

    def table(dist):
        onehot = (_t5_bucket(dist).reshape(1, ta * ta)
                  == jnp.arange(N_BUCKETS, dtype=jnp.int32)[:, None]).astype(F32)
        return jnp.dot(shifted, onehot, precision=lax.Precision.HIGHEST).reshape(nh, ta, ta)

    return table(rr - cc), table(rr - cc + ta)


SB_SUB = 128
SB_M = 8
SB_GS = 4
SB_CH = 64


def _sb_kernel(q_ref, k_ref, v_ref, tri_ref, o_ref, z_ref, lb_ref, hl_ref, both_ref, a_ref,
               acc_ref, run_ref, *, m, gs):
    n_sub = SB_SUB
    ch = SB_CH
    rows = gs * m * n_sub
    nch = rows // ch
    base = pl.program_id(2) * m

    def key_start(r, n):
        return pl.multiple_of(jnp.maximum(base + r - n, 0) * n_sub, n_sub)

    def strict_mask(r0):
        row = (r0 % n_sub) + lax.broadcasted_iota(jnp.int32, (ch, n_sub), 0)
        col = lax.broadcasted_iota(jnp.int32, (ch, n_sub), 1)
        return col < row

    def sweep(n, first):
        for hh in range(gs):
            for r in range(m):
                i = hh * m + r
                q = q_ref[hh, r * n_sub:(r + 1) * n_sub, :]
                z_ref[i * n_sub:(i + 1) * n_sub, :] = _dot_nt(q, k_ref[hh, pl.ds(key_start(r, n), n_sub), :])

        def split_body(c, carry):
            r0 = pl.multiple_of(c * ch, ch)
            z = z_ref[pl.ds(r0, ch), :]
            sp = jnp.log(1.0 + jnp.exp(-jnp.abs(z)))
            lb_ref[pl.ds(r0, ch), :] = jnp.minimum(z, 0.0) - sp
            log_keep = -jnp.maximum(z, 0.0) - sp
            if first:
                log_keep = jnp.where(strict_mask(r0), log_keep, 0.0)
            hl_ref[pl.ds(r0, ch), :] = log_keep.astype(BF16)
            return carry

        lax.fori_loop(0, nch, split_body, 0, unroll=2)

        both_ref[...] = jnp.dot(hl_ref[...], tri_ref[...], preferred_element_type=F32)

        def weight_body(c, mx):
            r0 = pl.multiple_of(c * ch, ch)
            both = both_ref[pl.ds(r0, ch), :]
            later = both[:, :n_sub]
            rowsum = both[:, n_sub:]
            if first:
                run = rowsum
            else:
                valid = base + (r0 // n_sub) % m - n >= 0
                prev = run_ref[pl.ds(r0, ch), :]
                later = later + prev
                run = prev + jnp.where(valid, rowsum, 0.0)
            a = jnp.exp(lb_ref[pl.ds(r0, ch), :] + later)
            if first:
                a = jnp.where(strict_mask(r0), a, 0.0)
            else:
                a = jnp.where(valid, a, 0.0)
            a_ref[pl.ds(r0, ch), :] = a.astype(BF16)
            run_ref[pl.ds(r0, ch), :] = run
            return jnp.maximum(mx, run)

        mx = lax.fori_loop(0, nch, weight_body, jnp.full((ch, n_sub), -jnp.inf, F32), unroll=2)

        for hh in range(gs):
            for r in range(m):
                i = hh * m + r
                sl = slice(i * n_sub, (i + 1) * n_sub)
                pv = jnp.dot(a_ref[sl, :], v_ref[hh, pl.ds(key_start(r, n), n_sub), :],
                             preferred_element_type=F32)
                if first:
                    acc_ref[sl, :] = pv
                else:
                    acc_ref[sl, :] = acc_ref[sl, :] + pv
        return jnp.max(mx)

    mx0 = sweep(0, True)

    def cond(carry):
        n, mx = carry
        return jnp.logical_and(n <= base + m - 1, mx > SB_EXIT)

    def body(carry):
        n, _ = carry
        return n + 1, sweep(n, False)

    lax.while_loop(cond, body, (jnp.int32(1), mx0))

    for hh in range(gs):
        for r in range(m):
            i = hh * m + r
            o_ref[r * n_sub:(r + 1) * n_sub, hh * HEAD_DIM:(hh + 1) * HEAD_DIM] = (
                acc_ref[i * n_sub:(i + 1) * n_sub, :])


def _stick_breaking(p_main, b, s, hb, off_q, off_k, off_v):
    gs = SB_GS
    m = SB_M
    tq = SB_SUB * m
    assert s % tq == 0 and hb % gs == 0
    assert off_q % gs == 0 and off_k % gs == 0 and off_v % gs == 0
    nq = s // tq
    rows = gs * m * SB_SUB
    idx = jnp.arange(SB_SUB)
    tri = jnp.concatenate([(idx[:, None] > idx[None, :]).astype(BF16),
                           jnp.ones((SB_SUB, SB_SUB), BF16)], axis=1)
    return pl.pallas_call(
        functools.partial(_sb_kernel, m=m, gs=gs),
        grid=(b, hb // gs, nq),
        in_specs=[pl.BlockSpec((gs, tq, LANES), lambda bi, gi, qi: (off_q // gs + gi, bi * nq + qi, 0)),
                  pl.BlockSpec((gs, s, LANES), lambda bi, gi, qi: (off_k // gs + gi, bi, 0)),
                  pl.BlockSpec((gs, s, LANES), lambda bi, gi, qi: (off_v // gs + gi, bi, 0)),
                  pl.BlockSpec((SB_SUB, 2 * SB_SUB), lambda bi, gi, qi: (0, 0))],
        out_specs=pl.BlockSpec((tq, gs * HEAD_DIM), lambda bi, gi, qi: (bi * nq + qi, gi)),
        out_shape=jax.ShapeDtypeStruct((b * s, hb * HEAD_DIM), F32),
        scratch_shapes=[pltpu.VMEM((rows, LANES), F32),
                        pltpu.VMEM((rows, LANES), F32),
                        pltpu.VMEM((rows, LANES), BF16),
                        pltpu.VMEM((rows, 2 * LANES), F32),
                        pltpu.VMEM((rows, LANES), BF16),
                        pltpu.VMEM((rows, LANES), F32),
                        pltpu.VMEM((rows, LANES), F32)],
        compiler_params=_cparams(("parallel", "parallel", "arbitrary")),
        name="stick_breaking",
    )(p_main, p_main, p_main, tri)


def kernel(x, c, in_ln_g, in_ln_b, rel_bias, w_ada, b_ada, w_in, idx_kn_g, idx_kn_b,
           gn_sparse_g, gn_sb_g, w_out, ln1_g, ln1_b, w_up, w_down, ln2_g, ln2_b):
    b, s, d = x.shape
    depth = w_ada.shape[0]
    t = b * s
    n_heads = d // HEAD_DIM
    hs = n_heads // 2
    g = hs // 4
    hb = n_heads - hs
    ws, wkv, wsb = hs * HEAD_DIM, g * HEAD_DIM, hb * HEAD_DIM
    wiq = H_IDX * D_IDX
    topk = min(TOPK_MAX, s // 4)
    alpha = (2.0 * depth) ** 0.25
    qscale = HEAD_DIM ** -0.5

    off_aq = wiq // LANES
    off_ak = off_aq + hs
    off_av = off_ak + g
    off_bq = off_av + g
    off_bk = off_bq + hb
    off_bv = off_bk + hb

    mods = [_ada(c, w_ada[l], b_ada[l]).reshape(b * 6, 1, d) for l in range(depth)]
    h, u = _ln_mod(x.reshape(t, d), in_ln_g, in_ln_b, mods[0], s, 1, 0)
    toep0, toep1 = _toeplitz_bias(rel_bias, min(SP_TA, s))
    for l in range(depth):
        mod3 = mods[l]
        o = 0
        cols = {}
        for name, size in (("aq", ws), ("ak", wkv), ("av", wkv), ("bq", wsb), ("bk", wsb),
                           ("bv", wsb), ("iq", wiq), ("ik", D_IDX), ("iw", H_IDX)):
            cols[name] = w_in[l][:, o:o + size]
            o += size
        w_main = jnp.concatenate([cols["iq"], cols["aq"] * qscale, cols["ak"], cols["av"],
                                  cols["bq"] * qscale, cols["bk"], cols["bv"]], axis=1).astype(BF16)
        w_tail = jnp.pad(jnp.concatenate([cols["ik"], cols["iw"]], axis=1),
                         ((0, 0), (0, LANES - D_IDX - H_IDX))).astype(BF16)
        p_main = _matmul(u, w_main, BF16, head_major=True, name="in_proj")
        tail = _matmul(u, w_tail, F32, name="in_proj_tail")
        ke, ko = _ikprep(tail, idx_kn_g[l], idx_kn_b[l])
        maskb = _indexer(p_main, ke, ko, tail, b, s, topk)
        o_a = _sparse_attention(p_main, maskb, toep0, toep1, b, s, hs, g, off_aq, off_ak, off_av)
        o_b = _stick_breaking(p_main, b, s, hb, off_bq, off_bk, off_bv)
        xn = _rms_cat(o_a, o_b, gn_sparse_g[l], gn_sb_g[l])
        mixed = _matmul(xn, w_out[l].astype(BF16), F32, name="out_proj")
        h, u = _res_ln(h, mixed, mod3, s, 2, ln1_g[l], ln1_b[l], alpha,
                       mod3_u=mod3, idx_sc=4, idx_sh=3)
        hmid = _matmul(u, w_up[l].astype(BF16), BF16, act="relu2", name="mlp_up")
        y = _matmul(hmid, w_down[l].astype(BF16), F32, name="mlp_down")
        if l + 1 < depth:
            h, u = _res_ln(h, y, mod3, s, 5, ln2_g[l], ln2_b[l], alpha,
                           mod3_u=mods[l + 1], idx_sc=1, idx_sh=0)
        else:
            h = _res_ln(h, y, mod3, s, 5, ln2_g[l], ln2_b[l], alpha)
    return h.reshape(b, s, d)
```

```python
import functools
import math

import jax
import jax.numpy as jnp
from jax import lax
from jax.experimental import pallas as pl
from jax.experimental.pallas import tpu as pltpu

HEAD_DIM = 128
H_IDX = 32
D_IDX = 64
TOPK_MAX = 256
N_BUCKETS = 32
MAX_DISTANCE = 128
LN_EPS = 1e-5

LANES = 128
SUBLANES = 8
VMEM_LIMIT = 56 * 1024 * 1024
NEG_BIG = -1e30
INT_MIN = -2147483648
SB_EXIT = -110.0

F32 = jnp.float32
BF16 = jnp.bfloat16


def _cparams(sem):
    return pltpu.CompilerParams(dimension_semantics=sem, vmem_limit_bytes=VMEM_LIMIT)


def _dot_nt(a, b):
    return lax.dot_general(a, b, (((1,), (1,)), ((), ())), preferred_element_type=F32)


def _ada_kernel(c_ref, w_ref, b_ref, o_ref, cs_ref, *, nb, d, tn):
    @pl.when(pl.program_id(0) == 0)
    def _():
        cv = c_ref[...]
        cs_ref[...] = cv * jax.nn.sigmoid(cv)

    nj = tn // LANES

    def body(k, accs):
        k8 = pl.multiple_of(k * SUBLANES, SUBLANES)
        out = list(accs)
        cs = [cs_ref[b, pl.ds(k8, SUBLANES), :] for b in range(nb)]
        for j in range(nj):
            wk = w_ref[pl.ds(k8, SUBLANES), j * LANES:(j + 1) * LANES]
            for b in range(nb):
                out[b * nj + j] = out[b * nj + j] + wk * cs[b]
        return tuple(out)

    init = tuple(jnp.zeros((SUBLANES, LANES), F32) for _ in range(nb * nj))
    accs = lax.fori_loop(0, d // SUBLANES, body, init, unroll=4)
    for b in range(nb):
        row = jnp.concatenate(
            [jnp.sum(accs[b * nj + j], axis=0, keepdims=True) for j in range(nj)], axis=1)
        o_ref[b:b + 1, :] = row + b_ref[...]


def _ada(c, w, bias):
    nb, d = c.shape
    n = w.shape[1]
    tn = 512 if n % 512 == 0 else LANES
    cb = jnp.broadcast_to(c[:, :, None], (nb, d, LANES))
    return pl.pallas_call(
        functools.partial(_ada_kernel, nb=nb, d=d, tn=tn),
        grid=(n // tn,),
        in_specs=[pl.BlockSpec((nb, d, LANES), lambda j: (0, 0, 0)),
                  pl.BlockSpec((d, tn), lambda j: (0, j)),
                  pl.BlockSpec((1, tn), lambda j: (0, j))],
        out_specs=pl.BlockSpec((nb, tn), lambda j: (0, j)),
        out_shape=jax.ShapeDtypeStruct((nb, n), F32),
        scratch_shapes=[pltpu.VMEM((nb, d, LANES), F32)],
        compiler_params=_cparams(("arbitrary",)),
        name="ada_mod",
    )(cb, w, bias.reshape(1, n))


def _layer_norm_rows(x, g, b):
    mu = jnp.mean(x, axis=-1, keepdims=True)
    xc = x - mu
    var = jnp.mean(xc * xc, axis=-1, keepdims=True)
    return xc * lax.rsqrt(var + LN_EPS) * g + b


def _ln_mod_kernel(x_ref, g_ref, b_ref, sc_ref, sh_ref, h_ref, u_ref):
    h = _layer_norm_rows(x_ref[...], g_ref[...], b_ref[...])
    h_ref[...] = h
    u_ref[...] = (h * (1.0 + sc_ref[0]) + sh_ref[0]).astype(BF16)


def _mod_spec(d, rows_per_batch_blocks, idx):
    return pl.BlockSpec((1, 1, d), lambda i: ((i // rows_per_batch_blocks) * 6 + idx, 0, 0))


def _ln_mod(x2, g, b, mod3, s, idx_sc, idx_sh):
    t, d = x2.shape
    tr = min(256, s)
    nbb = s // tr
    row = pl.BlockSpec((tr, d), lambda i: (i, 0))
    vec = pl.BlockSpec((1, d), lambda i: (0, 0))
    return pl.pallas_call(
        _ln_mod_kernel,
        grid=(t // tr,),
        in_specs=[row, vec, vec, _mod_spec(d, nbb, idx_sc), _mod_spec(d, nbb, idx_sh)],
        out_specs=[row, row],
        out_shape=[jax.ShapeDtypeStruct((t, d), F32), jax.ShapeDtypeStruct((t, d), BF16)],
        compiler_params=_cparams(("parallel",)),
        name="ln_mod",
    )(x2, g.reshape(1, d), b.reshape(1, d), mod3, mod3)


def _res_ln_kernel(h_ref, y_ref, gate_ref, g_ref, b_ref, sc_ref, sh_ref, *out_refs, alpha, with_u):
    h = _layer_norm_rows(alpha * h_ref[...] + gate_ref[0] * y_ref[...], g_ref[...], b_ref[...])
    out_refs[0][...] = h
    if with_u:
        out_refs[1][...] = (h * (1.0 + sc_ref[0]) + sh_ref[0]).astype(BF16)


def _res_ln(h, y, mod3, s, idx_gate, g, b, alpha, mod3_u=None, idx_sc=None, idx_sh=None):
    t, d = h.shape
    tr = min(256, s)
    nbb = s // tr
    with_u = mod3_u is not None
    if not with_u:
        mod3_u, idx_sc, idx_sh = mod3, idx_gate, idx_gate
    row = pl.BlockSpec((tr, d), lambda i: (i, 0))
    vec = pl.BlockSpec((1, d), lambda i: (0, 0))
    out_specs = [row, row] if with_u else [row]
    out_shape = [jax.ShapeDtypeStruct((t, d), F32)]
    if with_u:
        out_shape.append(jax.ShapeDtypeStruct((t, d), BF16))
    outs = pl.pallas_call(
        functools.partial(_res_ln_kernel, alpha=alpha, with_u=with_u),
        grid=(t // tr,),
        in_specs=[row, row, _mod_spec(d, nbb, idx_gate), vec, vec,
                  _mod_spec(d, nbb, idx_sc), _mod_spec(d, nbb, idx_sh)],
        out_specs=out_specs,
        out_shape=out_shape,
        compiler_params=_cparams(("parallel",)),
        name="res_ln",
    )(h, y, mod3, g.reshape(1, d), b.reshape(1, d), mod3_u, mod3_u)
    return outs if with_u else outs[0]


def _rms_cat_kernel(a_ref, b_ref, ga_ref, gb_ref, o_ref, *, wa):
    def rms(x, g):
        ms = jnp.mean(x * x, axis=-1, keepdims=True)
        return (x * lax.rsqrt(ms + LN_EPS) * g).astype(BF16)
    o_ref[:, :wa] = rms(a_ref[...], ga_ref[...])
    o_ref[:, wa:] = rms(b_ref[...], gb_ref[...])


def _rms_cat(oa, ob, ga, gb):
    t, wa = oa.shape
    wb = ob.shape[1]
    tr = min(512, t)
    return pl.pallas_call(
        functools.partial(_rms_cat_kernel, wa=wa),
        grid=(t // tr,),
        in_specs=[pl.BlockSpec((tr, wa), lambda i: (i, 0)), pl.BlockSpec((tr, wb), lambda i: (i, 0)),
                  pl.BlockSpec((1, wa), lambda i: (0, 0)), pl.BlockSpec((1, wb), lambda i: (0, 0))],
        out_specs=pl.BlockSpec((tr, wa + wb), lambda i: (i, 0)),
        out_shape=jax.ShapeDtypeStruct((t, wa + wb), BF16),
        compiler_params=_cparams(("parallel",)),
        name="rms_cat",
    )(oa, ob, ga.reshape(1, wa), gb.reshape(1, wb))


def _ikprep_kernel(t_ref, g_ref, b_ref, ke_ref, ko_ref):
    x = t_ref[...]
    lane = lax.broadcasted_iota(jnp.int32, x.shape, 1)
    is_k = lane < D_IDX
    mu = jnp.sum(jnp.where(is_k, x, 0.0), axis=-1, keepdims=True) * (1.0 / D_IDX)
    xc = jnp.where(is_k, x - mu, 0.0)
    var = jnp.sum(xc * xc, axis=-1, keepdims=True) * (1.0 / D_IDX)
    y = xc * lax.rsqrt(var + LN_EPS) * g_ref[...] + b_ref[...]
    ke_ref[...] = y.astype(BF16)
    ko_ref[...] = pltpu.roll(y, D_IDX, axis=1).astype(BF16)


def _ikprep(tail, g, b):
    t = tail.shape[0]
    tr = min(512, t)
    pad = lambda v: jnp.pad(v, (0, LANES - D_IDX)).reshape(1, LANES)
    row = pl.BlockSpec((tr, LANES), lambda i: (i, 0))
    vec = pl.BlockSpec((1, LANES), lambda i: (0, 0))
    return pl.pallas_call(
        _ikprep_kernel,
        grid=(t // tr,),
        in_specs=[row, vec, vec],
        out_specs=[row, row],
        out_shape=[jax.ShapeDtypeStruct((t, LANES), BF16)] * 2,
        compiler_params=_cparams(("parallel",)),
        name="ikprep",
    )(tail, pad(g), pad(b))


MM_TK_MAX = 4096


def _mm_kernel(a_ref, w_ref, o_ref, *, act, head_major, nk):
    r = jnp.dot(a_ref[...], w_ref[...], preferred_element_type=F32)
    if nk > 1:
        kk = pl.program_id(2)

        @pl.when(kk == 0)
        def _():
            o_ref[...] = r

        @pl.when(kk > 0)
        def _():
            o_ref[...] += r
        return
    if act == "relu2":
        r = jnp.square(jnp.maximum(r, 0.0))
    if head_major:
        for cblk in range(o_ref.shape[0]):
            o_ref[cblk] = r[:, cblk * LANES:(cblk + 1) * LANES].astype(o_ref.dtype)
    else:
        o_ref[...] = r.astype(o_ref.dtype)


def _mm_tiles(m, k, n):
    tm = min(1024, m)
    tn = 1024 if n % 1024 == 0 else (512 if n % 512 == 0 else LANES)
    tk = min(k, MM_TK_MAX)
    return tm, min(tn, n), tk


def _matmul(a, w, out_dtype, act=None, head_major=False, name="matmul"):
    m, k = a.shape
    n = w.shape[1]
    tm, tn, tk = _mm_tiles(m, k, n)
    nk = k // tk
    assert m % tm == 0 and n % tn == 0 and k % tk == 0
    assert nk == 1 or (act is None and not head_major and out_dtype == F32)
    if head_major:
        out_spec = pl.BlockSpec((tn // LANES, tm, LANES), lambda i, j, kk: (j, i, 0))
        out_shape = jax.ShapeDtypeStruct((n // LANES, m, LANES), out_dtype)
    else:
        out_spec = pl.BlockSpec((tm, tn), lambda i, j, kk: (i, j))
        out_shape = jax.ShapeDtypeStruct((m, n), out_dtype)
    return pl.pallas_call(
        functools.partial(_mm_kernel, act=act, head_major=head_major, nk=nk),
        grid=(m // tm, n // tn, nk),
        in_specs=[pl.BlockSpec((tm, tk), lambda i, j, kk: (i, kk)),
                  pl.BlockSpec((tk, tn), lambda i, j, kk: (kk, j))],
        out_specs=out_spec,
        out_shape=out_shape,
        compiler_params=_cparams(("parallel", "arbitrary", "arbitrary")),
        name=name,
    )(a, w)


IDX_TQ = 128
IDX_LC = 512


def _indexer_kernel(iq_ref, ke_ref, ko_ref, tail_ref, o_ref, zs_ref, keys_ref, wb_ref,
                    *, topk, s, w_scale):
    tq, lc = IDX_TQ, IDX_LC
    npair = H_IDX // 2
    qi = pl.program_id(1)
    t0 = qi * tq

    tl = tail_ref[...]
    for h in range(H_IDX):
        col = tl[:, D_IDX + h:D_IDX + h + 1] * w_scale
        wb_ref[h] = jnp.broadcast_to(col, (tq, LANES))

    a = iq_ref[...].reshape(npair * tq, LANES)
    nchunks = (t0 + tq + lc - 1) // lc

    def chunk_body(c, carry):
        c0 = pl.multiple_of(c * lc, lc)
        zs_ref[0] = _dot_nt(a, ke_ref[pl.ds(c0, lc), :])
        zs_ref[1] = _dot_nt(a, ko_ref[pl.ds(c0, lc), :])
        for r in range(tq // SUBLANES):
            rows = slice(r * SUBLANES, (r + 1) * SUBLANES)
            acc = jnp.zeros((SUBLANES, lc), F32)
            for p in range(npair):
                for par in range(2):
                    z = zs_ref[par, p * tq + r * SUBLANES:p * tq + (r + 1) * SUBLANES, :]
                    w = jnp.tile(wb_ref[2 * p + par, rows, :], (1, lc // LANES))
                    acc = acc + jnp.maximum(z, 0.0) * w
            t_idx = t0 + r * SUBLANES + lax.broadcasted_iota(jnp.int32, (SUBLANES, lc), 0)
            s_idx = c0 + lax.broadcasted_iota(jnp.int32, (SUBLANES, lc), 1)
            bits = pltpu.bitcast(acc, jnp.int32)
            key = bits ^ ((bits >> 31) & jnp.int32(0x7FFFFFFF))
            keys_ref[rows, pl.ds(c0, lc)] = jnp.where(s_idx <= t_idx, key, jnp.int32(INT_MIN))
        return carry

    lax.fori_loop(0, nchunks, chunk_body, 0)

    nl = nchunks * (lc // LANES)

    def bit_cond(carry):
        i, _, n_ge = carry
        return jnp.logical_and(i < 32, jnp.max(jnp.abs(n_ge - float(topk))) > 0.0)

    def bit_body(carry):
        i, ans, n_ge = carry
        cand = ans + lax.shift_left(jnp.int32(1), jnp.int32(31) - i)

        def cnt_body(c, cnt):
            c0 = pl.multiple_of(c * lc, lc)
            for j in range(lc // LANES):
                k = keys_ref[:, pl.ds(c0 + j * LANES, LANES)]
                cnt = cnt + jnp.where(k >= cand, 1.0, 0.0)
            return cnt

        cnt = lax.fori_loop(0, nchunks, cnt_body, jnp.zeros((tq, LANES), F32))
        tot = jnp.broadcast_to(jnp.sum(cnt, axis=1, keepdims=True), (tq, LANES))
        take = tot >= float(topk)
        return i + 1, jnp.where(take, cand, ans), jnp.where(take, tot, n_ge)

    n_all = jnp.broadcast_to((nl * LANES).astype(F32), (tq, LANES))
    _, ans, _ = lax.while_loop(
        bit_cond, bit_body, (jnp.int32(0), jnp.full((tq, LANES), INT_MIN, jnp.int32), n_all))
    thr = jnp.maximum(ans, jnp.int32(INT_MIN + 1))

    def sel_body(c, carry):
        cs = pl.multiple_of(c * LANES, LANES)
        k = keys_ref[:, pl.ds(cs, LANES)]
        o_ref[:, pl.ds(cs, LANES)] = jnp.where(k >= thr, 0.0, NEG_BIG).astype(BF16)
        return carry

    def fill_body(c, carry):
        cs = pl.multiple_of(c * LANES, LANES)
        o_ref[:, pl.ds(cs, LANES)] = jnp.full((tq, LANES), NEG_BIG, BF16)
        return carry

    lax.fori_loop(0, nl, sel_body, 0)
    lax.fori_loop(nl, s // LANES, fill_body, 0)


def _indexer(p_main, ke, ko, tail, b, s, topk):
    tq = IDX_TQ
    nq = s // tq
    npair = H_IDX // 2
    w_scale = (H_IDX ** -0.5) * (D_IDX ** -0.5)
    return pl.pallas_call(
        functools.partial(_indexer_kernel, topk=topk, s=s, w_scale=w_scale),
        grid=(b, nq),
        in_specs=[pl.BlockSpec((npair, tq, LANES), lambda bi, qi: (0, bi * nq + qi, 0)),
                  pl.BlockSpec((s, LANES), lambda bi, qi: (bi, 0)),
                  pl.BlockSpec((s, LANES), lambda bi, qi: (bi, 0)),
                  pl.BlockSpec((tq, LANES), lambda bi, qi: (bi * nq + qi, 0))],
        out_specs=pl.BlockSpec((tq, s), lambda bi, qi: (bi * nq + qi, 0)),
        out_shape=jax.ShapeDtypeStruct((b * s, s), BF16),
        scratch_shapes=[pltpu.VMEM((2, npair * tq, IDX_LC), F32),
                        pltpu.VMEM((tq, s), jnp.int32),
                        pltpu.VMEM((H_IDX, tq, LANES), F32)],
        compiler_params=_cparams(("parallel", "arbitrary")),
        name="indexer_topk",
    )(p_main, ke, ko, tail)


SP_TA = 256


def _sparse_kernel(qt_ref, kt_ref, q_ref, k_ref, v_ref, mb_ref, t0_ref, t1_ref, o_ref,
                   m_ref, l_ref, acc_ref, *, hs, g, ta):
    pair = pl.program_id(1)
    qi = qt_ref[pair]
    kj = kt_ref[pair]
    r = hs // g
    rows = r * ta

    @pl.when(kj == 0)
    def _():
        m_ref[...] = jnp.full(m_ref.shape, NEG_BIG, F32)
        l_ref[...] = jnp.zeros(l_ref.shape, F32)
        acc_ref[...] = jnp.zeros(acc_ref.shape, F32)

    def step(toep_ref):
        mb = mb_ref[...].astype(F32)
        for gi in range(g):
            sl = slice(gi * rows, (gi + 1) * rows)
            qg = q_ref[gi * r:(gi + 1) * r].reshape(rows, HEAD_DIM)
            sc = _dot_nt(qg, k_ref[gi]).reshape(r, ta, ta) + mb[None]
            if toep_ref is not None:
                sc = sc + toep_ref[gi * r:(gi + 1) * r]
            sc = sc.reshape(rows, ta)
            m_prev = m_ref[sl]
            m_next = jnp.maximum(m_prev, jnp.max(sc, axis=1, keepdims=True))
            p = jnp.exp(sc - jnp.tile(m_next, (1, ta // LANES)))
            alpha = jnp.exp(m_prev - m_next)
            l_ref[sl] = alpha * l_ref[sl] + jnp.sum(p, axis=1, keepdims=True)
            m_ref[sl] = m_next
            acc_ref[sl] = acc_ref[sl] * alpha + jnp.dot(p.astype(BF16), v_ref[gi],
                                                         preferred_element_type=F32)

    @pl.when(kj < qi - 1)
    def _():
        step(None)

    @pl.when(kj == qi - 1)
    def _():
        step(t1_ref)

    @pl.when(kj == qi)
    def _():
        step(t0_ref)
        for h in range(hs):
            hsl = slice(h * ta, (h + 1) * ta)
            o_ref[:, h * HEAD_DIM:(h + 1) * HEAD_DIM] = acc_ref[hsl] / l_ref[hsl]


def _sparse_attention(p_main, maskb, toep0, toep1, b, s, hs, g, off_q, off_k, off_v):
    ta = min(SP_TA, s)
    nq = s // ta
    assert off_q % hs == 0 and off_k % g == 0 and off_v % g == 0
    pairs = [(qi, kj) for qi in range(nq) for kj in range(qi + 1)]
    qt = jnp.asarray([p[0] for p in pairs], jnp.int32)
    kt = jnp.asarray([p[1] for p in pairs], jnp.int32)
    grid_spec = pltpu.PrefetchScalarGridSpec(
        num_scalar_prefetch=2,
        grid=(b, len(pairs)),
        in_specs=[pl.BlockSpec((hs, ta, LANES), lambda bi, p, qt, kt: (off_q // hs, bi * nq + qt[p], 0)),
                  pl.BlockSpec((g, ta, LANES), lambda bi, p, qt, kt: (off_k // g, bi * nq + kt[p], 0)),
                  pl.BlockSpec((g, ta, LANES), lambda bi, p, qt, kt: (off_v // g, bi * nq + kt[p], 0)),
                  pl.BlockSpec((ta, ta), lambda bi, p, qt, kt: (bi * nq + qt[p], kt[p])),
                  pl.BlockSpec((hs, ta, ta), lambda bi, p, qt, kt: (0, 0, 0)),
                  pl.BlockSpec((hs, ta, ta), lambda bi, p, qt, kt: (0, 0, 0))],
        out_specs=pl.BlockSpec((ta, hs * HEAD_DIM), lambda bi, p, qt, kt: (bi * nq + qt[p], 0)),
        scratch_shapes=[pltpu.VMEM((hs * ta, LANES), F32)] * 3,
    )
    return pl.pallas_call(
        functools.partial(_sparse_kernel, hs=hs, g=g, ta=ta),
        grid_spec=grid_spec,
        out_shape=jax.ShapeDtypeStruct((b * s, hs * HEAD_DIM), F32),
        compiler_params=_cparams(("parallel", "arbitrary")),
        name="sparse_attn",
    )(qt, kt, p_main, p_main, p_main, maskb, toep0, toep1)


def _t5_bucket(dist):
    n = jnp.maximum(dist, 0)
    max_exact = N_BUCKETS // 2
    nf = jnp.maximum(n, 1).astype(F32)
    large = max_exact + (jnp.log(nf / max_exact) / math.log(MAX_DISTANCE / max_exact)
                         * (N_BUCKETS - max_exact)).astype(jnp.int32)
    large = jnp.minimum(large, N_BUCKETS - 1)
    return jnp.where(n < max_exact, n, large)


def _toeplitz_bias(rel_bias, ta):
    assert ta >= MAX_DISTANCE
    nh = rel_bias.shape[1]
    rr = jnp.arange(ta, dtype=jnp.int32)[:, None]
    cc = jnp.arange(ta, dtype=jnp.int32)[None, :]
    far = rel_bias[_t5_bucket(jnp.int32(2 * ta))]
    shifted = (rel_bias - far[None, :]).T

    def table(dist):
        onehot = (_t5_bucket(dist).reshape(1, ta * ta)
                  == jnp.arange(N_BUCKETS, dtype=jnp.int32)[:, None]).astype(F32)
        return jnp.dot(shifted, onehot, precision=lax.Precision.HIGHEST).reshape(nh, ta, ta)

    return table(rr - cc), table(rr - cc + ta)


SB_SUB = 128
SB_M = 4
SB_GS = 4
SB_CH = 64


def _sb_kernel(q_ref, k_ref, v_ref, tri_ref, o_ref, z_ref, lb_ref, hl_ref, both_ref, a_ref,
               acc_ref, run_ref, *, m, gs):
    n_sub = SB_SUB
    ch = SB_CH
    rows = gs * m * n_sub
    nch = rows // ch
    base = pl.program_id(2) * m

    def key_start(r, n):
        return pl.multiple_of(jnp.maximum(base + r - n, 0) * n_sub, n_sub)

    def strict_mask(r0):
        row = (r0 % n_sub) + lax.broadcasted_iota(jnp.int32, (ch, n_sub), 0)
        col = lax.broadcasted_iota(jnp.int32, (ch, n_sub), 1)
        return col < row

    def sweep(n, first):
        for hh in range(gs):
            for r in range(m):
                i = hh * m + r
                q = q_ref[hh, r * n_sub:(r + 1) * n_sub, :]
                z_ref[i * n_sub:(i + 1) * n_sub, :] = _dot_nt(q, k_ref[hh, pl.ds(key_start(r, n), n_sub), :])

        def split_body(c, carry):
            r0 = pl.multiple_of(c * ch, ch)
            z = z_ref[pl.ds(r0, ch), :]
            sp = jnp.log(1.0 + jnp.exp(-jnp.abs(z)))
            lb_ref[pl.ds(r0, ch), :] = jnp.minimum(z, 0.0) - sp
            log_keep = -jnp.maximum(z, 0.0) - sp
            if first:
                log_keep = jnp.where(strict_mask(r0), log_keep, 0.0)
            hl_ref[pl.ds(r0, ch), :] = log_keep.astype(BF16)
            return carry

        lax.fori_loop(0, nch, split_body, 0, unroll=2)

        both_ref[...] = jnp.dot(hl_ref[...], tri_ref[...], preferred_element_type=F32)

        def weight_body(c, mx):
            r0 = pl.multiple_of(c * ch, ch)
            both = both_ref[pl.ds(r0, ch), :]
            later = both[:, :n_sub]
            rowsum = both[:, n_sub:]
            if first:
                run = rowsum
            else:
                valid = base + (r0 // n_sub) % m - n >= 0
                prev = run_ref[pl.ds(r0, ch), :]
                later = later + prev
                run = prev + jnp.where(valid, rowsum, 0.0)
            a = jnp.exp(lb_ref[pl.ds(r0, ch), :] + later)
            if first:
                a = jnp.where(strict_mask(r0), a, 0.0)
            else:
                a = jnp.where(valid, a, 0.0)
            a_ref[pl.ds(r0, ch), :] = a.astype(BF16)
            run_ref[pl.ds(r0, ch), :] = run
            return jnp.maximum(mx, run)

        mx = lax.fori_loop(0, nch, weight_body, jnp.full((ch, n_sub), -jnp.inf, F32), unroll=2)

        for hh in range(gs):
            for r in range(m):
                i = hh * m + r
                sl = slice(i * n_sub, (i + 1) * n_sub)
                pv = jnp.dot(a_ref[sl, :], v_ref[hh, pl.ds(key_start(r, n), n_sub), :],
                             preferred_element_type=F32)
                if first:
                    acc_ref[sl, :] = pv
                else:
                    acc_ref[sl, :] = acc_ref[sl, :] + pv
        return jnp.max(mx)

    mx0 = sweep(0, True)

    def cond(carry):
        n, mx = carry
        return jnp.logical_and(n <= base + m - 1, mx > SB_EXIT)

    def body(carry):
        n, _ = carry
        return n + 1, sweep(n, False)

    lax.while_loop(cond, body, (jnp.int32(1), mx0))

    for hh in range(gs):
        for r in range(m):
            i = hh * m + r
            o_ref[r * n_sub:(r + 1) * n_sub, hh * HEAD_DIM:(hh + 1) * HEAD_DIM] = (
                acc_ref[i * n_sub:(i + 1) * n_sub, :])


def _stick_breaking(p_main, b, s, hb, off_q, off_k, off_v):
    gs = SB_GS
    m = SB_M
    tq = SB_SUB * m
    assert s % tq == 0 and hb % gs == 0
    assert off_q % gs == 0 and off_k % gs == 0 and off_v % gs == 0
    nq = s // tq
    rows = gs * m * SB_SUB
    idx = jnp.arange(SB_SUB)
    tri = jnp.concatenate([(idx[:, None] > idx[None, :]).astype(BF16),
                           jnp.ones((SB_SUB, SB_SUB), BF16)], axis=1)
    return pl.pallas_call(
        functools.partial(_sb_kernel, m=m, gs=gs),
        grid=(b, hb // gs, nq),
        in_specs=[pl.BlockSpec((gs, tq, LANES), lambda bi, gi, qi: (off_q // gs + gi, bi * nq + qi, 0)),
                  pl.BlockSpec((gs, s, LANES), lambda bi, gi, qi: (off_k // gs + gi, bi, 0)),
                  pl.BlockSpec((gs, s, LANES), lambda bi, gi, qi: (off_v // gs + gi, bi, 0)),
                  pl.BlockSpec((SB_SUB, 2 * SB_SUB), lambda bi, gi, qi: (0, 0))],
        out_specs=pl.BlockSpec((tq, gs * HEAD_DIM), lambda bi, gi, qi: (bi * nq + qi, gi)),
        out_shape=jax.ShapeDtypeStruct((b * s, hb * HEAD_DIM), F32),
        scratch_shapes=[pltpu.VMEM((rows, LANES), F32),
                        pltpu.VMEM((rows, LANES), F32),
                        pltpu.VMEM((rows, LANES), BF16),
                        pltpu.VMEM((rows, 2 * LANES), F32),
                        pltpu.VMEM((rows, LANES), BF16),
                        pltpu.VMEM((rows, LANES), F32),
                        pltpu.VMEM((rows, LANES), F32)],
        compiler_params=_cparams(("parallel", "parallel", "arbitrary")),
        name="stick_breaking",
    )(p_main, p_main, p_main, tri)


def kernel(x, c, in_ln_g, in_ln_b, rel_bias, w_ada, b_ada, w_in, idx_kn_g, idx_kn_b,
           gn_sparse_g, gn_sb_g, w_out, ln1_g, ln1_b, w_up, w_down, ln2_g, ln2_b):
    b, s, d = x.shape
    depth = w_ada.shape[0]
    t = b * s
    n_heads = d // HEAD_DIM
    hs = n_heads // 2
    g = hs // 4
    hb = n_heads - hs
    ws, wkv, wsb = hs * HEAD_DIM, g * HEAD_DIM, hb * HEAD_DIM
    wiq = H_IDX * D_IDX
    topk = min(TOPK_MAX, s // 4)
    alpha = (2.0 * depth) ** 0.25
    qscale = HEAD_DIM ** -0.5

    off_aq = wiq // LANES
    off_ak = off_aq + hs
    off_av = off_ak + g
    off_bq = off_av + g
    off_bk = off_bq + hb
    off_bv = off_bk + hb

    mods = [_ada(c, w_ada[l], b_ada[l]).reshape(b * 6, 1, d) for l in range(depth)]
    h, u = _ln_mod(x.reshape(t, d), in_ln_g, in_ln_b, mods[0], s, 1, 0)
    toep0, toep1 = _toeplitz_bias(rel_bias, min(SP_TA, s))
    for l in range(depth):
        mod3 = mods[l]
        o = 0
        cols = {}
        for name, size in (("aq", ws), ("ak", wkv), ("av", wkv), ("bq", wsb), ("bk", wsb),
                           ("bv", wsb), ("iq", wiq), ("ik", D_IDX), ("iw", H_IDX)):
            cols[name] = w_in[l][:, o:o + size]
            o += size
        w_main = jnp.concatenate([cols["iq"], cols["aq"] * qscale, cols["ak"], cols["av"],
                                  cols["bq"] * qscale, cols["bk"], cols["bv"]], axis=1).astype(BF16)
        w_tail = jnp.pad(jnp.concatenate([cols["ik"], cols["iw"]], axis=1),
                         ((0, 0), (0, LANES - D_IDX - H_IDX))).astype(BF16)
        p_main = _matmul(u, w_main, BF16, head_major=True, name="in_proj")
        tail = _matmul(u, w_tail, F32, name="in_proj_tail")
        ke, ko = _ikprep(tail, idx_kn_g[l], idx_kn_b[l])
        maskb = _indexer(p_main, ke, ko, tail, b, s, topk)
        o_a = _sparse_attention(p_main, maskb, toep0, toep1, b, s, hs, g, off_aq, off_ak, off_av)
        o_b = _stick_breaking(p_main, b, s, hb, off_bq, off_bk, off_bv)
        xn = _rms_cat(o_a, o_b, gn_sparse_g[l], gn_sb_g[l])
        mixed = _matmul(xn, w_out[l].astype(BF16), F32, name="out_proj")
        h, u = _res_ln(h, mixed, mod3, s, 2, ln1_g[l], ln1_b[l], alpha,
                       mod3_u=mod3, idx_sc=4, idx_sh=3)
        hmid = _matmul(u, w_up[l].astype(BF16), BF16, act="relu2", name="mlp_up")
        y = _matmul(hmid, w_down[l].astype(BF16), F32, name="mlp_down")
        if l + 1 < depth:
            h, u = _res_ln(h, y, mod3, s, 5, ln2_g[l], ln2_b[l], alpha,
                           mod3_u=mods[l + 1], idx_sc=1, idx_sh=0)
        else:
            h = _res_ln(h, y, mod3, s, 5, ln2_g[l], ln2_b[l], alpha)
    return h.reshape(b, s, d)
```
